```python
import jax, jax.numpy as jnp
from jax import lax
import numpy as np

D_MODEL = 1024
BATCH = 1
SEQ = 16384
DEPTH = 4

N_MIXERS = 4
NORM_EPS = 1e-6
ROPE_THETA = 10000.0
LIN_CHUNK = 64
ATTN_BLOCK = 128
HGRN_EXPAND = 128
HGRN_HEADS = D_MODEL // HGRN_EXPAND
HGRN_DK = HGRN_EXPAND
HGRN_DV = D_MODEL // HGRN_HEADS
GLA_HEADS = 4
GLA_KEY_DIM = D_MODEL // 2
GLA_VALUE_DIM = D_MODEL
GLA_DK = GLA_KEY_DIM // GLA_HEADS
GLA_DV = GLA_VALUE_DIM // GLA_HEADS
GLA_GATE_RANK = 16
GLA_GATE_NORMALIZER = 16.0
RET_HEADS = 8
RET_DK = D_MODEL // RET_HEADS
RET_DV = 2 * D_MODEL // RET_HEADS
MLA_HEADS = 8
MLA_Q_LORA = 384
MLA_KV_LORA = 128
MLA_NOPE = 128
MLA_ROPE = 64
MLA_V = 128
MLA_QK = MLA_NOPE + MLA_ROPE
FFN_HIDDEN = -(-8 * D_MODEL // (3 * 256)) * 256

kernel_name = "hybrid_hgrn2_gla_retnet_mla_trunk"


def _rms_norm(x, gain=None):
    xf = x.astype(jnp.float32)
    y = xf * lax.rsqrt(jnp.mean(jnp.square(xf), axis=-1, keepdims=True) + NORM_EPS)
    if gain is not None:
        y = y * gain.astype(jnp.float32)
    return y.astype(x.dtype)


def _rope(x, positions):
    d = x.shape[-1]
    half = d // 2
    inv_freq = 1.0 / (ROPE_THETA ** (jnp.arange(half, dtype=jnp.float32) / half))
    ang = positions.astype(jnp.float32)[..., None] * inv_freq
    cos = jnp.cos(ang)[:, :, None, :]
    sin = jnp.sin(ang)[:, :, None, :]
    xf = x.astype(jnp.float32)
    x1, x2 = xf[..., :half], xf[..., half:]
    return jnp.concatenate([x1 * cos - x2 * sin, x2 * cos + x1 * sin], axis=-1).astype(x.dtype)


def _to_heads(t, n_heads):
    b, s, _ = t.shape
    return t.reshape(b, s, n_heads, -1).transpose(0, 2, 1, 3)


def _to_chunks(t, chunk):
    b, h, l, d = t.shape
    return jnp.moveaxis(t.reshape(b, h, l // chunk, chunk, d), 2, 0)


def _from_chunks(t):
    n, b, h, c, d = t.shape
    return jnp.moveaxis(t, 0, 2).reshape(b, h, n * c, d)


def _gated_linear_scan(q, k, v, log_f):
    b, h, _, dk = q.shape
    dv = v.shape[-1]
    f32 = jnp.float32
    qc, kc, vc, gc = (_to_chunks(t.astype(f32), LIN_CHUNK) for t in (q, k, v, log_f))
    causal = jnp.tril(jnp.ones((LIN_CHUNK, LIN_CHUNK), dtype=bool))[:, :, None]

    def step(state, inp):
        q_, k_, v_, g_ = inp
        cum = jnp.cumsum(g_, axis=2)
        diff = cum[:, :, :, None, :] - cum[:, :, None, :, :]
        decay = jnp.exp(jnp.where(causal, diff, -jnp.inf))
        scores = jnp.einsum('bhtk,bhsk,bhtsk->bhts', q_, k_, decay)
        o_intra = jnp.einsum('bhts,bhsv->bhtv', scores, v_)
        o_inter = jnp.einsum('bhtk,bhkv->bhtv', q_ * jnp.exp(cum), state)
        last = cum[:, :, -1:, :]
        new_state = (jnp.exp(last[:, :, 0, :])[..., None] * state
                     + jnp.einsum('bhsk,bhsv->bhkv', k_ * jnp.exp(last - cum), v_))
        return new_state, o_intra + o_inter

    s0 = jnp.zeros((b, h, dk, dv), f32)
    _, o = lax.scan(step, s0, (qc, kc, vc, gc))
    return _from_chunks(o)


def _retention_scan(q, k, v, log_gamma):
    b, h, _, dk = q.shape
    dv = v.shape[-1]
    f32 = jnp.float32
    qc, kc, vc = (_to_chunks(t.astype(f32), LIN_CHUNK) for t in (q, k, v))
    idx = jnp.arange(LIN_CHUNK, dtype=f32)
    lg = log_gamma[:, None]
    causal = jnp.tril(jnp.ones((LIN_CHUNK, LIN_CHUNK), dtype=bool))
    intra_decay = jnp.exp(jnp.where(causal, lg[:, :, None] * (idx[:, None] - idx[None, :]), -jnp.inf))
    q_decay = jnp.exp(lg * (idx + 1.0))[:, :, None]
    k_decay = jnp.exp(lg * (LIN_CHUNK - 1.0 - idx))[:, :, None]
    chunk_decay = jnp.exp(log_gamma * LIN_CHUNK)[:, None, None]

    def step(state, inp):
        q_, k_, v_ = inp
        scores = jnp.einsum('bhtk,bhsk->bhts', q_, k_) * intra_decay
        o = (jnp.einsum('bhts,bhsv->bhtv', scores, v_)
             + jnp.einsum('bhtk,bhkv->bhtv', q_ * q_decay, state))
        new_state = chunk_decay * state + jnp.einsum('bhsk,bhsv->bhkv', k_ * k_decay, v_)
        return new_state, o

    s0 = jnp.zeros((b, h, dk, dv), f32)
    _, o = lax.scan(step, s0, (qc, kc, vc))
    return _from_chunks(o)


def _causal_attention_blocked(q, k, v):
    b, s, h, dq = q.shape
    dv = v.shape[-1]
    nb = s // ATTN_BLOCK
    scale = dq ** -0.5
    qb = jnp.moveaxis(q.reshape(b, nb, ATTN_BLOCK, h, dq), 1, 0)
    kpos = jnp.arange(s)

    def one_block(args):
        q_blk, j = args
        qpos = j * ATTN_BLOCK + jnp.arange(ATTN_BLOCK)
        sc = jnp.einsum('bqhd,bkhd->bhqk', q_blk, k).astype(jnp.float32) * scale
        sc = jnp.where(kpos[None, :] <= qpos[:, None], sc, -jnp.inf)
        p = jax.nn.softmax(sc, axis=-1).astype(v.dtype)
        return jnp.einsum('bhqk,bkhd->bqhd', p, v)

    o = lax.map(one_block, (qb, jnp.arange(nb)))
    return jnp.moveaxis(o, 0, 1).reshape(b, s, h, dv)


def _hgrn2(h, w_in, g_norm, w_out, lower_bound):
    b, s, _ = h.shape
    q, f, i, g = jnp.split(h @ w_in, 4, axis=-1)
    q = jax.nn.silu(q) * (HGRN_DK ** -0.5)
    forget = lower_bound + (1.0 - lower_bound) * jax.nn.sigmoid(f.astype(jnp.float32))
    k = 1.0 - forget
    log_f = jnp.log(forget)
    o = _gated_linear_scan(_to_heads(q, HGRN_HEADS), _to_heads(k, HGRN_HEADS),
                           _to_heads(i, HGRN_HEADS), _to_heads(log_f, HGRN_HEADS))
    o = o.transpose(0, 2, 1, 3).astype(h.dtype)
    o = _rms_norm(o, g_norm) * jax.nn.silu(g.reshape(b, s, HGRN_HEADS, HGRN_DV))
    return o.reshape(b, s, -1) @ w_out


def _gla(h, w_in, w_gk_up, b_gk, g_norm, w_out):
    b, s, _ = h.shape
    q, k, v, g, gk_low = jnp.split(
        h @ w_in, [GLA_KEY_DIM, 2 * GLA_KEY_DIM, 2 * GLA_KEY_DIM + GLA_VALUE_DIM,
                   2 * GLA_KEY_DIM + 2 * GLA_VALUE_DIM], axis=-1)
    log_f = jax.nn.log_sigmoid((gk_low @ w_gk_up + b_gk).astype(jnp.float32)) / GLA_GATE_NORMALIZER
    q = q * (GLA_DK ** -0.5)
    o = _gated_linear_scan(_to_heads(q, GLA_HEADS), _to_heads(k, GLA_HEADS),
                           _to_heads(v, GLA_HEADS), _to_heads(log_f, GLA_HEADS))
    o = o.transpose(0, 2, 1, 3).astype(h.dtype)
    o = _rms_norm(o, g_norm) * jax.nn.silu(g.reshape(b, s, GLA_HEADS, GLA_DV))
    return o.reshape(b, s, -1) @ w_out


def _retnet(h, positions, w_in, w_out):
    b, s, _ = h.shape
    q, k, v, g = jnp.split(h @ w_in, [D_MODEL, 2 * D_MODEL, 4 * D_MODEL], axis=-1)
    q = _rope(q.reshape(b, s, RET_HEADS, RET_DK), positions)
    k = _rope(k.reshape(b, s, RET_HEADS, RET_DK), positions) * (RET_DK ** -0.5)
    log_gamma = jnp.log(1.0 - 2.0 ** (-5.0 - jnp.arange(RET_HEADS, dtype=jnp.float32)))
    o = _retention_scan(q.transpose(0, 2, 1, 3), k.transpose(0, 2, 1, 3),
                        _to_heads(v, RET_HEADS), log_gamma)
    o = _rms_norm(o.transpose(0, 2, 1, 3).astype(h.dtype))
    o = jax.nn.silu(g) * o.reshape(b, s, -1)
    return o @ w_out


def _mla(h, positions, w_in, g_q_lora, g_kv_lora, w_uq, w_ukv, g_qnorm, g_knorm, w_out):
    b, s, _ = h.shape
    c_q, c_kv, k_rope = jnp.split(h @ w_in, [MLA_Q_LORA, MLA_Q_LORA + MLA_KV_LORA], axis=-1)
    c_q = _rms_norm(c_q, g_q_lora)
    c_kv = _rms_norm(c_kv, g_kv_lora)
    q = (c_q @ w_uq).reshape(b, s, MLA_HEADS, MLA_QK)
    kv = (c_kv @ w_ukv).reshape(b, s, MLA_HEADS, MLA_NOPE + MLA_V)
    k_nope, v = kv[..., :MLA_NOPE], kv[..., MLA_NOPE:]
    k_rope = jnp.broadcast_to(k_rope[:, :, None, :], (b, s, MLA_HEADS, MLA_ROPE))
    k = jnp.concatenate([k_nope, k_rope], axis=-1)
    q = _rms_norm(q, g_qnorm)
    k = _rms_norm(k, g_knorm)
    q = jnp.concatenate([q[..., :MLA_NOPE], _rope(q[..., MLA_NOPE:], positions)], axis=-1)
    k = jnp.concatenate([k[..., :MLA_NOPE], _rope(k[..., MLA_NOPE:], positions)], axis=-1)
    o = _causal_attention_blocked(q, k, v)
    return o.reshape(b, s, -1) @ w_out


def _swiglu(h, w_gate_up, w_down):
    a, u = jnp.split(h @ w_gate_up, 2, axis=-1)
    return (jax.nn.silu(a) * u) @ w_down


def setup_inputs(seed: int = 0) -> dict:
    key = jax.random.key(seed)
    keys = iter(jax.random.split(key, 40))
    n_a, n_b, n_c, n_d = (len(range(m, DEPTH, N_MIXERS)) for m in range(N_MIXERS))
    out_gain = (2.0 * DEPTH) ** -0.5

    def w(shape, fan_in, gain=1.0):
        return jax.random.normal(next(keys), shape, jnp.float32) * (gain * fan_in ** -0.5)

    def gain(shape):
        return 1.0 + 0.02 * jax.random.normal(next(keys), shape, jnp.float32)

    x = jax.random.normal(next(keys), (BATCH, SEQ, D_MODEL), jnp.float32)
    positions = jnp.broadcast_to(jnp.arange(SEQ, dtype=jnp.int32), (BATCH, SEQ))
    gla_in = 2 * GLA_KEY_DIM + 2 * GLA_VALUE_DIM + GLA_GATE_RANK
    return {
        "x": x,
        "positions": positions,
        "norm_mix": gain((DEPTH, D_MODEL)),
        "norm_ffn": gain((DEPTH, D_MODEL)),
        "hgrn_w_in": w((n_a, D_MODEL, 4 * D_MODEL), D_MODEL),
        "hgrn_g_norm": gain((n_a, HGRN_DV)),
        "hgrn_w_out": w((n_a, D_MODEL, D_MODEL), D_MODEL, out_gain),
        "hgrn_lb_logits": 0.5 * jax.random.normal(next(keys), (DEPTH + 1, D_MODEL), jnp.float32),
        "gla_w_in": w((n_b, D_MODEL, gla_in), D_MODEL),
        "gla_w_gk_up": w((n_b, GLA_GATE_RANK, GLA_KEY_DIM), GLA_GATE_RANK),
        "gla_b_gk": 0.01 * jax.random.normal(next(keys), (n_b, GLA_KEY_DIM), jnp.float32),
        "gla_g_norm": gain((n_b, GLA_DV)),
        "gla_w_out": w((n_b, GLA_VALUE_DIM, D_MODEL), GLA_VALUE_DIM, out_gain),
        "ret_w_in": w((n_c, D_MODEL, 6 * D_MODEL), D_MODEL),
        "ret_w_out": w((n_c, 2 * D_MODEL, D_MODEL), 2 * D_MODEL, out_gain),
        "mla_w_in": w((n_d, D_MODEL, MLA_Q_LORA + MLA_KV_LORA + MLA_ROPE), D_MODEL),
        "mla_g_q_lora": gain((n_d, MLA_Q_LORA)),
        "mla_g_kv_lora": gain((n_d, MLA_KV_LORA)),
        "mla_w_uq": w((n_d, MLA_Q_LORA, MLA_HEADS * MLA_QK), MLA_Q_LORA),
        "mla_w_ukv": w((n_d, MLA_KV_LORA, MLA_HEADS * (MLA_NOPE + MLA_V)), MLA_KV_LORA),
        "mla_g_qnorm": gain((n_d, MLA_QK)),
        "mla_g_knorm": gain((n_d, MLA_QK)),
        "mla_w_out": w((n_d, MLA_HEADS * MLA_V, D_MODEL), MLA_HEADS * MLA_V, out_gain),
        "ffn_w_gate_up": w((DEPTH, D_MODEL, 2 * FFN_HIDDEN), D_MODEL),
        "ffn_w_down": w((DEPTH, FFN_HIDDEN, D_MODEL), FFN_HIDDEN, out_gain),
    }


def reference(x, positions, norm_mix, norm_ffn,
              hgrn_w_in, hgrn_g_norm, hgrn_w_out, hgrn_lb_logits,
              gla_w_in, gla_w_gk_up, gla_b_gk, gla_g_norm, gla_w_out,
              ret_w_in, ret_w_out,
              mla_w_in, mla_g_q_lora, mla_g_kv_lora, mla_w_uq, mla_w_ukv,
              mla_g_qnorm, mla_g_knorm, mla_w_out,
              ffn_w_gate_up, ffn_w_down):
    lower_bounds = jnp.cumsum(jax.nn.softmax(hgrn_lb_logits.astype(jnp.float32), axis=0), axis=0)
    for i in range(DEPTH):
        mixer, j = i % N_MIXERS, i // N_MIXERS
        h = _rms_norm(x, norm_mix[i])
        if mixer == 0:
            y = _hgrn2(h, hgrn_w_in[j], hgrn_g_norm[j], hgrn_w_out[j], lower_bounds[i])
        elif mixer == 1:
            y = _gla(h, gla_w_in[j], gla_w_gk_up[j], gla_b_gk[j], gla_g_norm[j], gla_w_out[j])
        elif mixer == 2:
            y = _retnet(h, positions, ret_w_in[j], ret_w_out[j])
        else:
            y = _mla(h, positions, mla_w_in[j], mla_g_q_lora[j], mla_g_kv_lora[j], mla_w_uq[j],
                     mla_w_ukv[j], mla_g_qnorm[j], mla_g_knorm[j], mla_w_out[j])
        x = x + y.astype(x.dtype)
        h = _rms_norm(x, norm_ffn[i])
        x = x + _swiglu(h, ffn_w_gate_up[i], ffn_w_down[i]).astype(x.dtype)
    return x
```

```python
import functools
import math

import numpy as np
import jax
import jax.numpy as jnp
from jax import lax
from jax.experimental import pallas as pl
from jax.experimental.pallas import tpu as pltpu

F32 = jnp.float32
BF16 = jnp.bfloat16

D_MODEL = 1024
NORM_EPS = 1e-6
ROPE_THETA = 10000.0
HGRN_HEADS, HGRN_DK, HGRN_DV = 8, 128, 128
GLA_HEADS, GLA_DK, GLA_DV = 4, 128, 256
GLA_KEY_DIM, GLA_VALUE_DIM, GLA_GATE_RANK = 512, 1024, 16
GLA_GATE_NORMALIZER = 16.0
RET_HEADS, RET_DK, RET_DV = 8, 128, 256
MLA_HEADS, MLA_Q_LORA, MLA_KV_LORA = 8, 384, 128
MLA_NOPE, MLA_ROPE, MLA_V = 128, 64, 128
MLA_QK = MLA_NOPE + MLA_ROPE
FFN_HIDDEN = 2816

LANE = 128
VMEM_LIMIT_BYTES = 56 * 1024 * 1024

SCAN_CHUNK = 128
PROJ_TM, PROJ_TN = 1024, 1024
FFN_TM = 256
MLA_TM = 256
ATTN_TQ, ATTN_TK = 512, 512

_NT = (((1,), (1,)), ((), ()))
_TN = (((0,), (0,)), ((), ()))


def _cparams(n_axes):
    return pltpu.CompilerParams(dimension_semantics=("arbitrary",) * n_axes,
                                vmem_limit_bytes=VMEM_LIMIT_BYTES)


def _rms(x, width=None):
    width = x.shape[-1] if width is None else width
    ss = jnp.sum(x * x, axis=-1, keepdims=True)
    return x * lax.rsqrt(ss * (1.0 / width) + NORM_EPS)


def _silu(x):
    return x * jax.nn.sigmoid(x)


def _rmsnorm_kernel(x_ref, g_ref, o_ref):
    o_ref[...] = (_rms(x_ref[...]) * g_ref[...]).astype(o_ref.dtype)


def _rmsnorm(x, gain, tm=1024):
    s, d = x.shape
    tm = min(tm, s)
    return pl.pallas_call(
        _rmsnorm_kernel,
        grid=(s // tm,),
        in_specs=[pl.BlockSpec((tm, d), lambda i: (i, 0)),
                  pl.BlockSpec((1, d), lambda i: (0, 0))],
        out_specs=pl.BlockSpec((tm, d), lambda i: (i, 0)),
        out_shape=jax.ShapeDtypeStruct((s, d), BF16),
        compiler_params=_cparams(1),
        name="rmsnorm",
    )(x, gain.reshape(1, d))


def _proj_kernel(h_ref, w_ref, o_ref):
    acc = jnp.dot(h_ref[...], w_ref[...], preferred_element_type=F32)
    for c in range(acc.shape[1] // LANE):
        o_ref[c] = acc[:, c * LANE:(c + 1) * LANE]


def _proj(h, w, tn):
    s, k = h.shape
    n = w.shape[1]
    tm = min(PROJ_TM, s)
    return pl.pallas_call(
        _proj_kernel,
        grid=(n // tn, s // tm),
        in_specs=[pl.BlockSpec((tm, k), lambda j, i: (i, 0)),
                  pl.BlockSpec((k, tn), lambda j, i: (0, j))],
        out_specs=pl.BlockSpec((tn // LANE, tm, LANE), lambda j, i: (j, i, 0)),
        out_shape=jax.ShapeDtypeStruct((n // LANE, s, LANE), F32),
        compiler_params=_cparams(2),
        name="proj",
    )(h, w)


def _scan_tables(chunk):
    n_levels = int(math.log2(chunk))
    t = np.arange(chunk)[:, None]
    r = np.arange(chunk)[None, :]
    blocks = []
    level = np.full((chunk, chunk), -1, np.int32)
    level[np.arange(chunk), np.arange(chunk)] = n_levels
    for l in range(n_levels):
        m = 1 << l
        b = (t // (2 * m)) * (2 * m) + m - 1
        upper = t > b
        w = np.where(upper, (r > b) & (r <= t), (r > t) & (r <= b))
        blocks.append(w)
        same = (t // (2 * m)) == (r // (2 * m))
        level[same & (t % (2 * m) >= m) & (r % (2 * m) < m)] = l
    blocks.append(r <= t)
    blocks.append(r > t)
    w = np.concatenate(blocks, axis=0).astype(np.float32)
    wcat = np.concatenate([w, w], axis=1)
    return jnp.asarray(wcat, BF16), jnp.asarray(level)


def _gated_chunk(q, k, v, g, wcat_ref, lv_ref, st_ref):
    c = q.shape[0]
    n_levels = int(math.log2(c))
    g_hi = g.astype(BF16)
    g_lo = (g - g_hi.astype(F32)).astype(BF16)
    gcat = jnp.concatenate([g_hi, g_lo], axis=0)
    dall = jnp.dot(wcat_ref[...], gcat, preferred_element_type=F32)
    lv = lv_ref[...]
    scores = jnp.where(lv == n_levels,
                       lax.dot_general(q.astype(BF16), k.astype(BF16), _NT,
                                       preferred_element_type=F32), 0.0)
    for l in range(n_levels):
        d = dall[l * c:(l + 1) * c]
        e = jnp.exp(jnp.minimum(d, -d))
        s = lax.dot_general((q * e).astype(BF16), (k * e).astype(BF16), _NT,
                            preferred_element_type=F32)
        scores = jnp.where(lv == l, s, scores)
    cum = dall[n_levels * c:(n_levels + 1) * c]
    rev = dall[(n_levels + 1) * c:(n_levels + 2) * c]
    q_in = (q * jnp.exp(cum)).astype(BF16)
    k_out = (k * jnp.exp(rev)).astype(BF16)
    vb = v.astype(BF16)
    st = st_ref[...]
    o = jnp.dot(scores.astype(BF16), vb, preferred_element_type=F32)
    o = o + lax.dot_general(q_in, st.astype(BF16), _NT, preferred_element_type=F32)
    st_ref[...] = (st * jnp.exp(cum[c - 1:c, :])
                   + lax.dot_general(vb, k_out, _TN, preferred_element_type=F32))
    return o


def _reset_state(st_ref):
    @pl.when(pl.program_id(1) == 0)
    def _():
        st_ref[...] = jnp.zeros_like(st_ref)


def _wide(ref):
    if ref.shape[0] == 1:
        return ref[0]
    return jnp.concatenate([ref[i] for i in range(ref.shape[0])], axis=1)


def _hgrn_kernel(q_ref, f_ref, i_ref, gt_ref, lb_ref, gn_ref, wcat_ref, lv_ref, o_ref, st_ref):
    _reset_state(st_ref)
    q = _silu(q_ref[0]) * (HGRN_DK ** -0.5)
    lb = lb_ref[0]
    forget = lb + (1.0 - lb) * jax.nn.sigmoid(f_ref[0])
    k = 1.0 - forget
    g = jnp.log(forget)
    o = _gated_chunk(q, k, i_ref[0], g, wcat_ref, lv_ref, st_ref)
    o = _rms(o) * gn_ref[...] * _silu(gt_ref[0])
    o_ref[...] = o.astype(o_ref.dtype)


def _hgrn_scan(p, lower_bound, g_norm):
    s = p.shape[1]
    c = min(SCAN_CHUNK, s)
    h = HGRN_HEADS
    wcat, level = _scan_tables(c)
    blk = lambda off: pl.BlockSpec((1, c, LANE), lambda hd, t: (off + hd, t, 0))
    const = lambda a: pl.BlockSpec(a.shape, lambda hd, t: (0,) * a.ndim)
    gn = g_norm.reshape(1, HGRN_DV)
    return pl.pallas_call(
        _hgrn_kernel,
        grid=(h, s // c),
        in_specs=[blk(0), blk(h), blk(2 * h), blk(3 * h),
                  pl.BlockSpec((1, 1, LANE), lambda hd, t: (hd, 0, 0)),
                  const(gn), const(wcat), const(level)],
        out_specs=pl.BlockSpec((c, HGRN_DV), lambda hd, t: (t, hd)),
        out_shape=jax.ShapeDtypeStruct((s, h * HGRN_DV), BF16),
        scratch_shapes=[pltpu.VMEM((HGRN_DV, HGRN_DK), F32)],
        compiler_params=_cparams(2),
        name="hgrn_scan",
    )(p, p, p, p, lower_bound.reshape(h, 1, LANE), gn, wcat, level)


def _log_sigmoid(z):
    return jnp.minimum(z, 0.0) - jnp.log(1.0 + jnp.exp(-jnp.abs(z)))


def _gla_kernel(q_ref, k_ref, v_ref, gt_ref, low_ref, wup_ref, b_ref, gn_ref, wcat_ref, lv_ref,
                o_ref, st_ref):
    _reset_state(st_ref)
    q = q_ref[0] * (GLA_DK ** -0.5)
    z = jnp.dot(low_ref[0].astype(BF16), wup_ref[0], preferred_element_type=F32) + b_ref[0]
    g = _log_sigmoid(z) * (1.0 / GLA_GATE_NORMALIZER)
    o = _gated_chunk(q, k_ref[0], _wide(v_ref), g, wcat_ref, lv_ref, st_ref)
    o = _rms(o) * gn_ref[...] * _silu(_wide(gt_ref))
    o_ref[...] = o.astype(o_ref.dtype)


def _gla_scan(p, w_up, b_gk, g_norm):
    s = p.shape[1]
    c = min(SCAN_CHUNK, s)
    h = GLA_HEADS
    wcat, level = _scan_tables(c)
    nv = GLA_DV // LANE
    blk = lambda off: pl.BlockSpec((1, c, LANE), lambda hd, t: (off + hd, t, 0))
    blk2 = lambda off: pl.BlockSpec((nv, c, LANE), lambda hd, t: (off // nv + hd, t, 0))
    const = lambda a: pl.BlockSpec(a.shape, lambda hd, t: (0,) * a.ndim)
    gn = g_norm.reshape(1, GLA_DV)
    return pl.pallas_call(
        _gla_kernel,
        grid=(h, s // c),
        in_specs=[blk(0), blk(h), blk2(2 * h), blk2(2 * h + nv * h),
                  pl.BlockSpec((1, c, LANE), lambda hd, t: (2 * h + 2 * nv * h, t, 0)),
                  pl.BlockSpec((1, LANE, LANE), lambda hd, t: (hd, 0, 0)),
                  pl.BlockSpec((1, 1, LANE), lambda hd, t: (hd, 0, 0)),
                  const(gn), const(wcat), const(level)],
        out_specs=pl.BlockSpec((c, GLA_DV), lambda hd, t: (t, hd)),
        out_shape=jax.ShapeDtypeStruct((s, h * GLA_DV), BF16),
        scratch_shapes=[pltpu.VMEM((GLA_DV, GLA_DK), F32)],
        compiler_params=_cparams(2),
        name="gla_scan",
    )(p, p, p, p, p, w_up, b_gk.reshape(h, 1, LANE), gn, wcat, level)


def _ret_kernel(q_ref, k_ref, v_ref, gt_ref, cos_ref, sin_ref, dm_ref, qd_ref, kd_ref, cd_ref,
                o_ref, st_ref):
    _reset_state(st_ref)
    cos = cos_ref[...]
    sin = sin_ref[...]
    q = q_ref[0]
    k = k_ref[0]
    half = RET_DK // 2
    q = q * cos + pltpu.roll(q, half, 1) * sin
    k = (k * cos + pltpu.roll(k, half, 1) * sin) * (RET_DK ** -0.5)
    vb = _wide(v_ref).astype(BF16)
    scores = lax.dot_general(q.astype(BF16), k.astype(BF16), _NT,
                             preferred_element_type=F32) * dm_ref[0]
    st = st_ref[...]
    o = jnp.dot(scores.astype(BF16), vb, preferred_element_type=F32)
    o = o + lax.dot_general((q * qd_ref[0]).astype(BF16), st.astype(BF16), _NT,
                            preferred_element_type=F32)
    st_ref[...] = st * cd_ref[0] + lax.dot_general(vb, (k * kd_ref[0]).astype(BF16), _TN,
                                                   preferred_element_type=F32)
    o = _rms(o) * _silu(_wide(gt_ref))
    o_ref[...] = o.astype(o_ref.dtype)


def _rope_tables(positions, dim, pad_to):
    half = dim // 2
    inv_freq = 1.0 / (ROPE_THETA ** (jnp.arange(half, dtype=F32) / half))
    ang = positions.astype(F32)[:, None] * inv_freq
    cos, sin = jnp.cos(ang), jnp.sin(ang)
    pad = jnp.zeros((positions.shape[0], pad_to // 2 - half), F32)
    cos_t = jnp.concatenate([cos, pad, cos, pad], axis=1)
    sin_t = jnp.concatenate([-sin, pad, sin, pad], axis=1)
    return cos_t, sin_t


def _ret_scan(p, positions):
    s = p.shape[1]
    c = min(SCAN_CHUNK, s)
    h = RET_HEADS
    nv = RET_DV // LANE
    cos_t, sin_t = _rope_tables(positions, RET_DK, LANE)
    log_gamma = jnp.log(1.0 - 2.0 ** (-5.0 - jnp.arange(h, dtype=F32)))
    idx = jnp.arange(c, dtype=F32)
    causal = jnp.tril(jnp.ones((c, c), dtype=bool))
    lg = log_gamma[:, None, None]
    dm = jnp.exp(jnp.where(causal, lg * (idx[:, None] - idx[None, :]), -jnp.inf))
    ones = jnp.ones((1, 1, LANE), F32)
    qd = jnp.exp(lg * (idx[None, :, None] + 1.0)) * ones
    kd = jnp.exp(lg * (c - 1.0 - idx[None, :, None])) * ones
    cd = jnp.exp(lg * float(c)) * ones
    blk = lambda off: pl.BlockSpec((1, c, LANE), lambda hd, t: (off + hd, t, 0))
    blk2 = lambda off: pl.BlockSpec((nv, c, LANE), lambda hd, t: (off // nv + hd, t, 0))
    tab = pl.BlockSpec((c, LANE), lambda hd, t: (t, 0))
    per_head = lambda a: pl.BlockSpec((1,) + a.shape[1:], lambda hd, t: (hd, 0, 0))
    return pl.pallas_call(
        _ret_kernel,
        grid=(h, s // c),
        in_specs=[blk(0), blk(h), blk2(2 * h), blk2(2 * h + nv * h), tab, tab,
                  per_head(dm), per_head(qd), per_head(kd), per_head(cd)],
        out_specs=pl.BlockSpec((c, RET_DV), lambda hd, t: (t, hd)),
        out_shape=jax.ShapeDtypeStruct((s, h * RET_DV), BF16),
        scratch_shapes=[pltpu.VMEM((RET_DV, RET_DK), F32)],
        compiler_params=_cparams(2),
        name="ret_scan",
    )(p, p, p, p, cos_t, sin_t, dm, qd, kd, cd)


MLA_QK_PAD = 2 * LANE
MLA_IN_PAD = MLA_Q_LORA + MLA_KV_LORA + LANE


def _mla_prep_kernel(h_ref, win_ref, gq_ref, gkv_ref, wuq_ref, wukv_ref, gqn_ref, gkn_ref,
                     cos_ref, sin_ref, q_ref, k_ref, v_ref):
    nh = MLA_HEADS
    c = jnp.dot(h_ref[...], win_ref[...], preferred_element_type=F32)
    c_q = _rms(c[:, :MLA_Q_LORA]) * gq_ref[...]
    c_kv = _rms(c[:, MLA_Q_LORA:MLA_Q_LORA + MLA_KV_LORA]) * gkv_ref[...]
    k_rope = c[:, MLA_Q_LORA + MLA_KV_LORA:]
    qf = jnp.dot(c_q.astype(BF16), wuq_ref[...], preferred_element_type=F32)
    kvf = jnp.dot(c_kv.astype(BF16), wukv_ref[...], preferred_element_type=F32)
    cos = cos_ref[...]
    sin = sin_ref[...]
    gqn = gqn_ref[...]
    gkn = gkn_ref[...]
    kr_ss = jnp.sum(k_rope * k_rope, axis=-1, keepdims=True)
    scale = MLA_QK ** -0.5
    for hd in range(nh):
        q_n = qf[:, hd * LANE:(hd + 1) * LANE]
        q_r = qf[:, (nh + hd) * LANE:(nh + hd + 1) * LANE]
        ss = jnp.sum(q_n * q_n, axis=-1, keepdims=True) + jnp.sum(q_r * q_r, axis=-1, keepdims=True)
        r = lax.rsqrt(ss * (1.0 / MLA_QK) + NORM_EPS)
        q_n = q_n * r * gqn[:, :LANE]
        q_r = q_r * r * gqn[:, LANE:]
        q_r = q_r * cos + pltpu.roll(q_r, LANE // 2, 1) * sin
        q_ref[hd] = (jnp.concatenate([q_n, q_r], axis=1) * scale).astype(q_ref.dtype)
        k_n = kvf[:, hd * LANE:(hd + 1) * LANE]
        ss = jnp.sum(k_n * k_n, axis=-1, keepdims=True) + kr_ss
        r = lax.rsqrt(ss * (1.0 / MLA_QK) + NORM_EPS)
        k_n = k_n * r * gkn[:, :LANE]
        k_r = k_rope * r * gkn[:, LANE:]
        k_r = k_r * cos + pltpu.roll(k_r, LANE // 2, 1) * sin
        k_ref[hd] = jnp.concatenate([k_n, k_r], axis=1).astype(k_ref.dtype)
        v_ref[hd] = kvf[:, (nh + hd) * LANE:(nh + hd + 1) * LANE].astype(v_ref.dtype)


def _pad_rope_cols(w):
    half = MLA_ROPE // 2
    z = jnp.zeros(w.shape[:-1] + (LANE // 2 - half,), w.dtype)
    return jnp.concatenate([w[..., :half], z, w[..., half:], z], axis=-1)


def _mla_prep(h, positions, w_in, g_q_lora, g_kv_lora, w_uq, w_ukv, g_qnorm, g_knorm):
    s = h.shape[0]
    tm = min(MLA_TM, s)
    nh = MLA_HEADS
    lat = MLA_Q_LORA + MLA_KV_LORA
    w_in_p = jnp.concatenate([w_in[:, :lat], _pad_rope_cols(w_in[:, lat:])], axis=1).astype(BF16)
    wq = w_uq.reshape(MLA_Q_LORA, nh, MLA_QK)
    wq_nope = wq[:, :, :MLA_NOPE].reshape(MLA_Q_LORA, nh * LANE)
    wq_rope = _pad_rope_cols(wq[:, :, MLA_NOPE:]).reshape(MLA_Q_LORA, nh * LANE)
    w_uq_p = jnp.concatenate([wq_nope, wq_rope], axis=1).astype(BF16)
    wkv = w_ukv.reshape(MLA_KV_LORA, nh, MLA_NOPE + MLA_V)
    w_ukv_p = jnp.concatenate([wkv[:, :, :MLA_NOPE].reshape(MLA_KV_LORA, nh * LANE),
                               wkv[:, :, MLA_NOPE:].reshape(MLA_KV_LORA, nh * LANE)],
                              axis=1).astype(BF16)
    pad_gain = lambda g: jnp.concatenate([g[:MLA_NOPE], _pad_rope_cols(g[MLA_NOPE:])]).reshape(1, -1)
    cos_t, sin_t = _rope_tables(positions, MLA_ROPE, LANE)
    const = lambda a: pl.BlockSpec(a.shape, lambda i: (0,) * a.ndim)
    args = (w_in_p, g_q_lora.reshape(1, -1), g_kv_lora.reshape(1, -1), w_uq_p, w_ukv_p,
            pad_gain(g_qnorm), pad_gain(g_knorm))
    tab = pl.BlockSpec((tm, LANE), lambda i: (i, 0))
    return pl.pallas_call(
        _mla_prep_kernel,
        grid=(s // tm,),
        in_specs=[pl.BlockSpec((tm, D_MODEL), lambda i: (i, 0))] + [const(a) for a in args] + [tab, tab],
        out_specs=[pl.BlockSpec((nh, tm, MLA_QK_PAD), lambda i: (0, i, 0)),
                   pl.BlockSpec((nh, tm, MLA_QK_PAD), lambda i: (0, i, 0)),
                   pl.BlockSpec((nh, tm, MLA_V), lambda i: (0, i, 0))],
        out_shape=[jax.ShapeDtypeStruct((nh, s, MLA_QK_PAD), BF16),
                   jax.ShapeDtypeStruct((nh, s, MLA_QK_PAD), BF16),
                   jax.ShapeDtypeStruct((nh, s, MLA_V), BF16)],
        compiler_params=_cparams(1),
        name="mla_prep",
    )(h, *args, cos_t, sin_t)


def _flash_kernel(q_ref, k_ref, v_ref, o_ref, *, tq, tk):
    i = pl.program_id(1)
    q = q_ref[0]

    def update(carry, s, vb):
        m_prev, l_prev, acc = carry
        m_new = jnp.maximum(m_prev, jnp.max(s, axis=-1, keepdims=True))
        alpha = jnp.exp(m_prev - m_new)
        p = jnp.exp(s - m_new)
        l_new = alpha * l_prev + jnp.sum(p, axis=-1, keepdims=True)
        acc = alpha * acc + jnp.dot(p.astype(BF16), vb, preferred_element_type=F32)
        return m_new, l_new, acc

    def body(j, carry):
        off = pl.multiple_of(j * tk, tk)
        s = lax.dot_general(q, k_ref[0, pl.ds(off, tk), :], _NT, preferred_element_type=F32)
        return update(carry, s, v_ref[0, pl.ds(off, tk), :])

    init = (jnp.full((tq, 1), -jnp.inf, F32), jnp.zeros((tq, 1), F32), jnp.zeros((tq, MLA_V), F32))
    n_full = i * (tq // tk)
    carry = lax.fori_loop(0, n_full, body, init)
    row = lax.broadcasted_iota(jnp.int32, (tq, tk), 0)
    col = lax.broadcasted_iota(jnp.int32, (tq, tk), 1)
    for d in range(tq // tk):
        off = pl.multiple_of(i * tq + d * tk, tk)
        s = lax.dot_general(q, k_ref[0, pl.ds(off, tk), :], _NT, preferred_element_type=F32)
        s = jnp.where(col + d * tk <= row, s, -jnp.inf)
        carry = update(carry, s, v_ref[0, pl.ds(off, tk), :])
    _, l_fin, acc = carry
    o_ref[...] = (acc / l_fin).astype(o_ref.dtype)


def _flash(q, k, v):
    nh, s, _ = q.shape
    tq = min(ATTN_TQ, s)
    tk = min(ATTN_TK, tq)
    return pl.pallas_call(
        functools.partial(_flash_kernel, tq=tq, tk=tk),
        grid=(nh, s // tq),
        in_specs=[pl.BlockSpec((1, tq, MLA_QK_PAD), lambda h, i: (h, i, 0)),
                  pl.BlockSpec((1, s, MLA_QK_PAD), lambda h, i: (h, 0, 0)),
                  pl.BlockSpec((1, s, MLA_V), lambda h, i: (h, 0, 0))],
        out_specs=pl.BlockSpec((tq, MLA_V), lambda h, i: (i, h)),
        out_shape=jax.ShapeDtypeStruct((s, nh * MLA_V), BF16),
        compiler_params=_cparams(2),
        name="flash",
    )(q, k, v)


def _out_ffn_kernel(a_ref, x_ref, wo_ref, gf_ref, wgu_ref, wd_ref, gn_ref, xo_ref, ho_ref):
    x1 = x_ref[...] + jnp.dot(a_ref[...], wo_ref[...], preferred_element_type=F32)
    h = (_rms(x1) * gf_ref[...]).astype(BF16)
    au = jnp.dot(h, wgu_ref[...], preferred_element_type=F32)
    p = (_silu(au[:, :FFN_HIDDEN]) * au[:, FFN_HIDDEN:]).astype(BF16)
    x2 = x1 + jnp.dot(p, wd_ref[...], preferred_element_type=F32)
    xo_ref[...] = x2
    ho_ref[...] = (_rms(x2) * gn_ref[...]).astype(ho_ref.dtype)


def _out_ffn(a, x, w_out, g_ffn, w_gate_up, w_down, g_next):
    s, d = x.shape
    tm = min(FFN_TM, s)
    row = lambda width: pl.BlockSpec((tm, width), lambda i: (i, 0))
    resident = lambda arr: pl.BlockSpec(arr.shape, lambda i: (0, 0), pipeline_mode=pl.Buffered(1))
    gf = g_ffn.reshape(1, d)
    gn = g_next.reshape(1, d)
    return pl.pallas_call(
        _out_ffn_kernel,
        grid=(s // tm,),
        in_specs=[row(a.shape[1]), row(d), resident(w_out), resident(gf), resident(w_gate_up),
                  resident(w_down), resident(gn)],
        out_specs=[row(d), row(d)],
        out_shape=[jax.ShapeDtypeStruct((s, d), F32), jax.ShapeDtypeStruct((s, d), BF16)],
        compiler_params=_cparams(1),
        name="out_ffn",
    )(a, x, w_out, gf, w_gate_up, w_down, gn)


def kernel(x, positions, norm_mix, norm_ffn, hgrn_w_in, hgrn_g_norm, hgrn_w_out, hgrn_lb_logits,
           gla_w_in, gla_w_gk_up, gla_b_gk, gla_g_norm, gla_w_out, ret_w_in, ret_w_out, mla_w_in,
           mla_g_q_lora, mla_g_kv_lora, mla_w_uq, mla_w_ukv, mla_g_qnorm, mla_g_knorm, mla_w_out,
           ffn_w_gate_up, ffn_w_down):
    b, s, d = x.shape
    depth = norm_mix.shape[0]
    lower_bounds = jnp.cumsum(jax.nn.softmax(hgrn_lb_logits.astype(F32), axis=0), axis=0)
    outs = []
    for bi in range(b):
        xs = x[bi]
        pos = positions[bi]
        h = _rmsnorm(xs, norm_mix[0])
        for i in range(depth):
            mixer, j = i % 4, i // 4
            if mixer == 0:
                p = _proj(h, hgrn_w_in[j].astype(BF16), PROJ_TN)
                a = _hgrn_scan(p, lower_bounds[i], hgrn_g_norm[j])
                w_out = hgrn_w_out[j]
            elif mixer == 1:
                main = 2 * GLA_KEY_DIM + 2 * GLA_VALUE_DIM
                w_low = jnp.pad(gla_w_in[j][:, main:], ((0, 0), (0, LANE - GLA_GATE_RANK)))
                w_in = jnp.concatenate([gla_w_in[j][:, :main], w_low], axis=1).astype(BF16)
                p = _proj(h, w_in, w_in.shape[1] // 5)
                w_up = jnp.pad(gla_w_gk_up[j], ((0, LANE - GLA_GATE_RANK), (0, 0)))
                w_up = w_up.reshape(LANE, GLA_HEADS, GLA_DK).transpose(1, 0, 2).astype(BF16)
                a = _gla_scan(p, w_up, gla_b_gk[j], gla_g_norm[j])
                w_out = gla_w_out[j]
            elif mixer == 2:
                p = _proj(h, ret_w_in[j].astype(BF16), PROJ_TN)
                a = _ret_scan(p, pos)
                w_out = ret_w_out[j]
            else:
                q, k, v = _mla_prep(h, pos, mla_w_in[j], mla_g_q_lora[j], mla_g_kv_lora[j],
                                    mla_w_uq[j], mla_w_ukv[j], mla_g_qnorm[j], mla_g_knorm[j])
                a = _flash(q, k, v)
                w_out = mla_w_out[j]
            g_next = norm_mix[i + 1] if i + 1 < depth else jnp.ones((d,), F32)
            xs, h = _out_ffn(a, xs, w_out.astype(BF16), norm_ffn[i], ffn_w_gate_up[i].astype(BF16),
                             ffn_w_down[i].astype(BF16), g_next)
        outs.append(xs)
    return outs[0][None] if b == 1 else jnp.stack(outs, axis=0)
```

```python
import functools
import math

import numpy as np
import jax
import jax.numpy as jnp
from jax import lax
from jax.experimental import pallas as pl
from jax.experimental.pallas import tpu as pltpu

F32 = jnp.float32
BF16 = jnp.bfloat16

D_MODEL = 1024
NORM_EPS = 1e-6
ROPE_THETA = 10000.0
HGRN_HEADS, HGRN_DK, HGRN_DV = 8, 128, 128
GLA_HEADS, GLA_DK, GLA_DV = 4, 128, 256
GLA_KEY_DIM, GLA_VALUE_DIM, GLA_GATE_RANK = 512, 1024, 16
GLA_GATE_NORMALIZER = 16.0
RET_HEADS, RET_DK, RET_DV = 8, 128, 256
MLA_HEADS, MLA_Q_LORA, MLA_KV_LORA = 8, 384, 128
MLA_NOPE, MLA_ROPE, MLA_V = 128, 64, 128
MLA_QK = MLA_NOPE + MLA_ROPE
FFN_HIDDEN = 2816

LANE = 128
VMEM_LIMIT_BYTES = 56 * 1024 * 1024

SCAN_CHUNK = 128
PROJ_TM, PROJ_TN = 1024, 1024
FFN_TM = 256
MLA_TM = 256
ATTN_TILE = 512

_NT = (((1,), (1,)), ((), ()))
_TN = (((0,), (0,)), ((), ()))


def _cparams(n_axes):
    return pltpu.CompilerParams(dimension_semantics=("arbitrary",) * n_axes,
                                vmem_limit_bytes=VMEM_LIMIT_BYTES)


def _rms(x, width=None):
    width = x.shape[-1] if width is None else width
    ss = jnp.sum(x * x, axis=-1, keepdims=True)
    return x * lax.rsqrt(ss * (1.0 / width) + NORM_EPS)


def _silu(x):
    return x * jax.nn.sigmoid(x)


def _rmsnorm_kernel(x_ref, g_ref, o_ref):
    o_ref[...] = (_rms(x_ref[...]) * g_ref[...]).astype(o_ref.dtype)


def _rmsnorm(x, gain, tm=1024):
    s, d = x.shape
    tm = min(tm, s)
    return pl.pallas_call(
        _rmsnorm_kernel,
        grid=(s // tm,),
        in_specs=[pl.BlockSpec((tm, d), lambda i: (i, 0)),
                  pl.BlockSpec((1, d), lambda i: (0, 0))],
        out_specs=pl.BlockSpec((tm, d), lambda i: (i, 0)),
        out_shape=jax.ShapeDtypeStruct((s, d), BF16),
        compiler_params=_cparams(1),
        name="rmsnorm",
    )(x, gain.reshape(1, d))


def _proj_kernel(h_ref, w_ref, o_ref):
    acc = jnp.dot(h_ref[...], w_ref[...], preferred_element_type=F32)
    for c in range(acc.shape[1] // LANE):
        o_ref[c] = acc[:, c * LANE:(c + 1) * LANE]


def _proj(h, w, tn):
    s, k = h.shape
    n = w.shape[1]
    tm = min(PROJ_TM, s)
    return pl.pallas_call(
        _proj_kernel,
        grid=(n // tn, s // tm),
        in_specs=[pl.BlockSpec((tm, k), lambda j, i: (i, 0)),
                  pl.BlockSpec((k, tn), lambda j, i: (0, j))],
        out_specs=pl.BlockSpec((tn // LANE, tm, LANE), lambda j, i: (j, i, 0)),
        out_shape=jax.ShapeDtypeStruct((n // LANE, s, LANE), F32),
        compiler_params=_cparams(2),
        name="proj",
    )(h, w)


def _scan_tables(chunk):
    n_levels = int(math.log2(chunk))
    t = np.arange(chunk)[:, None]
    r = np.arange(chunk)[None, :]
    blocks = []
    level = np.full((chunk, chunk), -1, np.int32)
    level[np.arange(chunk), np.arange(chunk)] = n_levels
    for l in range(n_levels):
        m = 1 << l
        b = (t // (2 * m)) * (2 * m) + m - 1
        upper = t > b
        w = np.where(upper, (r > b) & (r <= t), (r > t) & (r <= b))
        blocks.append(w)
        same = (t // (2 * m)) == (r // (2 * m))
        level[same & (t % (2 * m) >= m) & (r % (2 * m) < m)] = l
    blocks.append(r <= t)
    blocks.append(r > t)
    w = np.concatenate(blocks, axis=0).astype(np.float32)
    wcat = np.concatenate([w, w], axis=1)
    return jnp.asarray(wcat, BF16), jnp.asarray(level)


def _gate_sums(g, wcat_ref):
    g_hi = g.astype(BF16)
    g_lo = (g - g_hi.astype(F32)).astype(BF16)
    gcat = jnp.concatenate([g_hi, g_lo], axis=0)
    return jnp.dot(wcat_ref[...], gcat, preferred_element_type=F32)


def _gated_chunk(q, k, v, dall, lv, st_ref):
    c = q.shape[0]
    n_levels = int(math.log2(c))
    scores = jnp.where(lv == n_levels,
                       lax.dot_general(q.astype(BF16), k.astype(BF16), _NT,
                                       preferred_element_type=F32), 0.0)
    for l in range(n_levels):
        d = dall[l * c:(l + 1) * c]
        e = jnp.exp(jnp.minimum(d, -d))
        s = lax.dot_general((q * e).astype(BF16), (k * e).astype(BF16), _NT,
                            preferred_element_type=F32)
        scores = jnp.where(lv == l, s, scores)
    cum = dall[n_levels * c:(n_levels + 1) * c]
    rev = dall[(n_levels + 1) * c:(n_levels + 2) * c]
    q_in = (q * jnp.exp(cum)).astype(BF16)
    k_out = (k * jnp.exp(rev)).astype(BF16)
    vb = v.astype(BF16)
    st = st_ref[...]
    o = jnp.dot(scores.astype(BF16), vb, preferred_element_type=F32)
    o = o + lax.dot_general(q_in, st.astype(BF16), _NT, preferred_element_type=F32)
    st_ref[...] = (st * jnp.exp(cum[c - 1:c, :])
                   + lax.dot_general(vb, k_out, _TN, preferred_element_type=F32))
    return o


def _reset_state(st_ref):
    @pl.when(pl.program_id(0) == 0)
    def _():
        st_ref[...] = jnp.zeros_like(st_ref)


def _cols(p_ref, first, n):
    if n == 1:
        return p_ref[first]
    return jnp.concatenate([p_ref[first + i] for i in range(n)], axis=1)


def _hgrn_kernel(p_ref, lb_ref, gn_ref, wcat_ref, lv_ref, o_ref, st_ref):
    _reset_state(st_ref)
    nh = HGRN_HEADS
    lb = lb_ref[...]
    forget = lb + (1.0 - lb) * jax.nn.sigmoid(_cols(p_ref, nh, nh))
    dall = _gate_sums(jnp.log(forget), wcat_ref)
    lv = lv_ref[...]
    for hd in range(nh):
        lanes = slice(hd * LANE, (hd + 1) * LANE)
        q = _silu(p_ref[hd]) * (HGRN_DK ** -0.5)
        o = _gated_chunk(q, 1.0 - forget[:, lanes], p_ref[2 * nh + hd], dall[:, lanes], lv,
                         st_ref.at[hd])
        o = _rms(o) * gn_ref[...] * _silu(p_ref[3 * nh + hd])
        o_ref[:, lanes] = o.astype(o_ref.dtype)


def _scan_call(body, p, consts, n_heads, dk, dv, name):
    nb, s, _ = p.shape
    c = min(SCAN_CHUNK, s)
    const = lambda a: pl.BlockSpec(a.shape, lambda t: (0,) * a.ndim)
    return pl.pallas_call(
        body,
        grid=(s // c,),
        in_specs=[pl.BlockSpec((nb, c, LANE), lambda t: (0, t, 0))] + [const(a) for a in consts],
        out_specs=pl.BlockSpec((c, n_heads * dv), lambda t: (t, 0)),
        out_shape=jax.ShapeDtypeStruct((s, n_heads * dv), BF16),
        scratch_shapes=[pltpu.VMEM((n_heads, dv, dk), F32)],
        compiler_params=_cparams(1),
        name=name,
    )(p, *consts)


def _hgrn_scan(p, lower_bound, g_norm):
    wcat, level = _scan_tables(min(SCAN_CHUNK, p.shape[1]))
    consts = (lower_bound.reshape(1, -1), g_norm.reshape(1, HGRN_DV), wcat, level)
    return _scan_call(_hgrn_kernel, p, consts, HGRN_HEADS, HGRN_DK, HGRN_DV, "hgrn_scan")


def _log_sigmoid(z):
    return jnp.minimum(z, 0.0) - jnp.log(1.0 + jnp.exp(-jnp.abs(z)))


def _gla_kernel(p_ref, wup_ref, b_ref, gn_ref, wcat_ref, lv_ref, o_ref, st_ref):
    _reset_state(st_ref)
    nh = GLA_HEADS
    nv = GLA_DV // LANE
    low = p_ref[2 * nh + 2 * nv * nh].astype(BF16)
    z = jnp.dot(low, wup_ref[...], preferred_element_type=F32) + b_ref[...]
    dall = _gate_sums(_log_sigmoid(z) * (1.0 / GLA_GATE_NORMALIZER), wcat_ref)
    lv = lv_ref[...]
    for hd in range(nh):
        q = p_ref[hd] * (GLA_DK ** -0.5)
        o = _gated_chunk(q, p_ref[nh + hd], _cols(p_ref, 2 * nh + nv * hd, nv),
                         dall[:, hd * LANE:(hd + 1) * LANE], lv, st_ref.at[hd])
        o = _rms(o) * gn_ref[...] * _silu(_cols(p_ref, 2 * nh + nv * nh + nv * hd, nv))
        o_ref[:, hd * GLA_DV:(hd + 1) * GLA_DV] = o.astype(o_ref.dtype)


def _gla_scan(p, w_up, b_gk, g_norm):
    wcat, level = _scan_tables(min(SCAN_CHUNK, p.shape[1]))
    consts = (w_up, b_gk.reshape(1, -1), g_norm.reshape(1, GLA_DV), wcat, level)
    return _scan_call(_gla_kernel, p, consts, GLA_HEADS, GLA_DK, GLA_DV, "gla_scan")


def _ret_kernel(p_ref, cos_ref, sin_ref, dm_ref, qd_ref, kd_ref, cd_ref, o_ref, st_ref):
    _reset_state(st_ref)
    nh = RET_HEADS
    nv = RET_DV // LANE
    cos = cos_ref[...]
    sin = sin_ref[...]
    half = RET_DK // 2
    for hd in range(nh):
        q = p_ref[hd]
        k = p_ref[nh + hd]
        q = q * cos + pltpu.roll(q, half, 1) * sin
        k = (k * cos + pltpu.roll(k, half, 1) * sin) * (RET_DK ** -0.5)
        vb = _cols(p_ref, 2 * nh + nv * hd, nv).astype(BF16)
        scores = lax.dot_general(q.astype(BF16), k.astype(BF16), _NT,
                                 preferred_element_type=F32) * dm_ref[hd]
        st = st_ref[hd]
        o = jnp.dot(scores.astype(BF16), vb, preferred_element_type=F32)
        o = o + lax.dot_general((q * qd_ref[hd]).astype(BF16), st.astype(BF16), _NT,
                                preferred_element_type=F32)
        st_ref[hd] = st * cd_ref[hd] + lax.dot_general(vb, (k * kd_ref[hd]).astype(BF16), _TN,
                                                       preferred_element_type=F32)
        o = _rms(o) * _silu(_cols(p_ref, 2 * nh + nv * nh + nv * hd, nv))
        o_ref[:, hd * RET_DV:(hd + 1) * RET_DV] = o.astype(o_ref.dtype)


def _rope_tables(positions, dim, pad_to):
    half = dim // 2
    inv_freq = 1.0 / (ROPE_THETA ** (jnp.arange(half, dtype=F32) / half))
    ang = positions.astype(F32)[:, None] * inv_freq
    cos, sin = jnp.cos(ang), jnp.sin(ang)
    pad = jnp.zeros((positions.shape[0], pad_to // 2 - half), F32)
    cos_t = jnp.concatenate([cos, pad, cos, pad], axis=1)
    sin_t = jnp.concatenate([-sin, pad, sin, pad], axis=1)
    return cos_t, sin_t


def _ret_scan(p, positions):
    nb, s, _ = p.shape
    c = min(SCAN_CHUNK, s)
    h = RET_HEADS
    cos_t, sin_t = _rope_tables(positions, RET_DK, LANE)
    log_gamma = jnp.log(1.0 - 2.0 ** (-5.0 - jnp.arange(h, dtype=F32)))
    idx = jnp.arange(c, dtype=F32)
    causal = jnp.tril(jnp.ones((c, c), dtype=bool))
    lg = log_gamma[:, None, None]
    dm = jnp.exp(jnp.where(causal, lg * (idx[:, None] - idx[None, :]), -jnp.inf))
    ones = jnp.ones((1, 1, LANE), F32)
    qd = jnp.exp(lg * (idx[None, :, None] + 1.0)) * ones
    kd = jnp.exp(lg * (c - 1.0 - idx[None, :, None])) * ones
    cd = jnp.exp(lg * float(c)) * ones
    tab = pl.BlockSpec((c, LANE), lambda t: (t, 0))
    const = lambda a: pl.BlockSpec(a.shape, lambda t: (0,) * a.ndim)
    return pl.pallas_call(
        _ret_kernel,
        grid=(s // c,),
        in_specs=[pl.BlockSpec((nb, c, LANE), lambda t: (0, t, 0)), tab, tab,
                  const(dm), const(qd), const(kd), const(cd)],
        out_specs=pl.BlockSpec((c, h * RET_DV), lambda t: (t, 0)),
        out_shape=jax.ShapeDtypeStruct((s, h * RET_DV), BF16),
        scratch_shapes=[pltpu.VMEM((h, RET_DV, RET_DK), F32)],
        compiler_params=_cparams(1),
        name="ret_scan",
    )(p, cos_t, sin_t, dm, qd, kd, cd)


MLA_QK_PAD = 2 * LANE


def _mla_prep_kernel(h_ref, win_ref, gq_ref, gkv_ref, wuq_ref, wukv_ref, gqn_ref, gkn_ref,
                     cos_ref, sin_ref, q_ref, k_ref, v_ref):
    nh = MLA_HEADS
    c = jnp.dot(h_ref[...], win_ref[...], preferred_element_type=F32)
    c_q = _rms(c[:, :MLA_Q_LORA]) * gq_ref[...]
    c_kv = _rms(c[:, MLA_Q_LORA:MLA_Q_LORA + MLA_KV_LORA]) * gkv_ref[...]
    k_rope = c[:, MLA_Q_LORA + MLA_KV_LORA:]
    qf = jnp.dot(c_q.astype(BF16), wuq_ref[...], preferred_element_type=F32)
    kvf = jnp.dot(c_kv.astype(BF16), wukv_ref[...], preferred_element_type=F32)
    cos = cos_ref[...]
    sin = sin_ref[...]
    gqn = gqn_ref[...]
    gkn = gkn_ref[...]
    kr_ss = jnp.sum(k_rope * k_rope, axis=-1, keepdims=True)
    scale = MLA_QK ** -0.5 * math.log2(math.e)
    for hd in range(nh):
        q_n = qf[:, hd * LANE:(hd + 1) * LANE]
        q_r = qf[:, (nh + hd) * LANE:(nh + hd + 1) * LANE]
        ss = jnp.sum(q_n * q_n, axis=-1, keepdims=True) + jnp.sum(q_r * q_r, axis=-1, keepdims=True)
        r = lax.rsqrt(ss * (1.0 / MLA_QK) + NORM_EPS)
        q_n = q_n * r * gqn[:, :LANE]
        q_r = q_r * r * gqn[:, LANE:]
        q_r = q_r * cos + pltpu.roll(q_r, LANE // 2, 1) * sin
        q_ref[hd] = (jnp.concatenate([q_n, q_r], axis=1) * scale).astype(q_ref.dtype)
        k_n = kvf[:, hd * LANE:(hd + 1) * LANE]
        ss = jnp.sum(k_n * k_n, axis=-1, keepdims=True) + kr_ss
        r = lax.rsqrt(ss * (1.0 / MLA_QK) + NORM_EPS)
        k_n = k_n * r * gkn[:, :LANE]
        k_r = k_rope * r * gkn[:, LANE:]
        k_r = k_r * cos + pltpu.roll(k_r, LANE // 2, 1) * sin
        k_ref[hd] = jnp.concatenate([k_n, k_r], axis=1).T.astype(k_ref.dtype)
        v_h = kvf[:, (nh + hd) * LANE:(nh + hd + 1) * LANE]
        v_ref[hd] = jnp.concatenate([v_h, jnp.ones_like(v_h)], axis=1).astype(v_ref.dtype)


def _pad_rope_cols(w):
    half = MLA_ROPE // 2
    z = jnp.zeros(w.shape[:-1] + (LANE // 2 - half,), w.dtype)
    return jnp.concatenate([w[..., :half], z, w[..., half:], z], axis=-1)


def _mla_prep(h, positions, w_in, g_q_lora, g_kv_lora, w_uq, w_ukv, g_qnorm, g_knorm):
    s = h.shape[0]
    tm = min(MLA_TM, s)
    nh = MLA_HEADS
    lat = MLA_Q_LORA + MLA_KV_LORA
    w_in_p = jnp.concatenate([w_in[:, :lat], _pad_rope_cols(w_in[:, lat:])], axis=1).astype(BF16)
    wq = w_uq.reshape(MLA_Q_LORA, nh, MLA_QK)
    wq_nope = wq[:, :, :MLA_NOPE].reshape(MLA_Q_LORA, nh * LANE)
    wq_rope = _pad_rope_cols(wq[:, :, MLA_NOPE:]).reshape(MLA_Q_LORA, nh * LANE)
    w_uq_p = jnp.concatenate([wq_nope, wq_rope], axis=1).astype(BF16)
    wkv = w_ukv.reshape(MLA_KV_LORA, nh, MLA_NOPE + MLA_V)
    w_ukv_p = jnp.concatenate([wkv[:, :, :MLA_NOPE].reshape(MLA_KV_LORA, nh * LANE),
                               wkv[:, :, MLA_NOPE:].reshape(MLA_KV_LORA, nh * LANE)],
                              axis=1).astype(BF16)
    pad_gain = lambda g: jnp.concatenate([g[:MLA_NOPE], _pad_rope_cols(g[MLA_NOPE:])]).reshape(1, -1)
    cos_t, sin_t = _rope_tables(positions, MLA_ROPE, LANE)
    const = lambda a: pl.BlockSpec(a.shape, lambda i: (0,) * a.ndim)
    args = (w_in_p, g_q_lora.reshape(1, -1), g_kv_lora.reshape(1, -1), w_uq_p, w_ukv_p,
            pad_gain(g_qnorm), pad_gain(g_knorm))
    tab = pl.BlockSpec((tm, LANE), lambda i: (i, 0))
    return pl.pallas_call(
        _mla_prep_kernel,
        grid=(s // tm,),
        in_specs=[pl.BlockSpec((tm, D_MODEL), lambda i: (i, 0))] + [const(a) for a in args] + [tab, tab],
        out_specs=[pl.BlockSpec((nh, tm, MLA_QK_PAD), lambda i: (0, i, 0)),
                   pl.BlockSpec((nh, MLA_QK_PAD, tm), lambda i: (0, 0, i)),
                   pl.BlockSpec((nh, tm, 2 * MLA_V), lambda i: (0, i, 0))],
        out_shape=[jax.ShapeDtypeStruct((nh, s, MLA_QK_PAD), BF16),
                   jax.ShapeDtypeStruct((nh, MLA_QK_PAD, s), BF16),
                   jax.ShapeDtypeStruct((nh, s, 2 * MLA_V), BF16)],
        compiler_params=_cparams(1),
        name="mla_prep",
    )(h, *args, cos_t, sin_t)


def _flash_kernel(q_ref, k_ref, v_ref, o_ref, s_ref, m_ref, acc_ref, *, tile):
    i = pl.program_id(1)
    m_ref[...] = jnp.full(m_ref.shape, -jnp.inf, F32)
    acc_ref[...] = jnp.zeros(acc_ref.shape, F32)

    def scores(u, j, dst):
        off = pl.multiple_of(j * tile, tile)
        s_ref[dst, u] = jnp.dot(q_ref[0, u * tile:(u + 1) * tile, :], k_ref[0, :, pl.ds(off, tile)],
                                preferred_element_type=F32)

    def accumulate(u, j, src, masked):
        off = pl.multiple_of(j * tile, tile)
        s = s_ref[src, u]
        if masked:
            row = lax.broadcasted_iota(jnp.int32, (tile, tile), 0)
            col = lax.broadcasted_iota(jnp.int32, (tile, tile), 1)
            s = jnp.where(col <= row, s, -jnp.inf)
        m_prev = m_ref[u]
        m_new = jnp.maximum(m_prev, jnp.max(s, axis=-1, keepdims=True))
        p = jnp.exp2(s - m_new).astype(BF16)
        pv = jnp.dot(p, v_ref[0, pl.ds(off, tile), :], preferred_element_type=F32)
        acc_ref[u] = jnp.exp2(m_prev - m_new) * acc_ref[u] + pv
        m_ref[u] = m_new

    for u in range(2):
        scores(u, 0, 0)

    def body(jj, carry):
        for u in range(2):
            scores(u, 2 * jj + 1, 1)
        for u in range(2):
            accumulate(u, 2 * jj, 0, False)
        for u in range(2):
            scores(u, 2 * jj + 2, 0)
        for u in range(2):
            accumulate(u, 2 * jj + 1, 1, False)
        return carry

    lax.fori_loop(0, i, body, 0)
    scores(1, 2 * i + 1, 1)
    accumulate(0, 2 * i, 0, True)
    accumulate(1, 2 * i, 0, False)
    accumulate(1, 2 * i + 1, 1, True)
    for u in range(2):
        acc = acc_ref[u]
        o_ref[u * tile:(u + 1) * tile, :] = (acc[:, :MLA_V] / acc[:, MLA_V:]).astype(o_ref.dtype)


def _flash(q, k, v):
    nh, s, _ = q.shape
    tile = min(ATTN_TILE, s // 2)
    tq = 2 * tile
    return pl.pallas_call(
        functools.partial(_flash_kernel, tile=tile),
        grid=(nh, s // tq),
        in_specs=[pl.BlockSpec((1, tq, MLA_QK_PAD), lambda h, i: (h, i, 0)),
                  pl.BlockSpec((1, MLA_QK_PAD, s), lambda h, i: (h, 0, 0)),
                  pl.BlockSpec((1, s, 2 * MLA_V), lambda h, i: (h, 0, 0))],
        out_specs=pl.BlockSpec((tq, MLA_V), lambda h, i: (i, h)),
        out_shape=jax.ShapeDtypeStruct((s, nh * MLA_V), BF16),
        scratch_shapes=[pltpu.VMEM((2, 2, tile, tile), F32),
                        pltpu.VMEM((2, tile, 1), F32),
                        pltpu.VMEM((2, tile, 2 * MLA_V), F32)],
        compiler_params=_cparams(2),
        name="flash",
    )(q, k, v)


def _out_ffn_kernel(a_ref, x_ref, wo_ref, gf_ref, wgu_ref, wd_ref, gn_ref, xo_ref, ho_ref):
    x1 = x_ref[...] + jnp.dot(a_ref[...], wo_ref[...], preferred_element_type=F32)
    h = (_rms(x1) * gf_ref[...]).astype(BF16)
    au = jnp.dot(h, wgu_ref[...], preferred_element_type=F32)
    p = (_silu(au[:, :FFN_HIDDEN]) * au[:, FFN_HIDDEN:]).astype(BF16)
    x2 = x1 + jnp.dot(p, wd_ref[...], preferred_element_type=F32)
    xo_ref[...] = x2
    ho_ref[...] = (_rms(x2) * gn_ref[...]).astype(ho_ref.dtype)


def _out_ffn(a, x, w_out, g_ffn, w_gate_up, w_down, g_next):
    s, d = x.shape
    tm = min(FFN_TM, s)
    row = lambda width: pl.BlockSpec((tm, width), lambda i: (i, 0))
    resident = lambda arr: pl.BlockSpec(arr.shape, lambda i: (0, 0), pipeline_mode=pl.Buffered(1))
    gf = g_ffn.reshape(1, d)
    gn = g_next.reshape(1, d)
    return pl.pallas_call(
        _out_ffn_kernel,
        grid=(s // tm,),
        in_specs=[row(a.shape[1]), row(d), resident(w_out), resident(gf), resident(w_gate_up),
                  resident(w_down), resident(gn)],
        out_specs=[row(d), row(d)],
        out_shape=[jax.ShapeDtypeStruct((s, d), F32), jax.ShapeDtypeStruct((s, d), BF16)],
        compiler_params=_cparams(1),
        name="out_ffn",
    )(a, x, w_out, gf, w_gate_up, w_down, gn)


def kernel(x, positions, norm_mix, norm_ffn, hgrn_w_in, hgrn_g_norm, hgrn_w_out, hgrn_lb_logits,
           gla_w_in, gla_w_gk_up, gla_b_gk, gla_g_norm, gla_w_out, ret_w_in, ret_w_out, mla_w_in,
           mla_g_q_lora, mla_g_kv_lora, mla_w_uq, mla_w_ukv, mla_g_qnorm, mla_g_knorm, mla_w_out,
           ffn_w_gate_up, ffn_w_down):
    b, s, d = x.shape
    depth = norm_mix.shape[0]
    lower_bounds = jnp.cumsum(jax.nn.softmax(hgrn_lb_logits.astype(F32), axis=0), axis=0)
    outs = []
    for bi in range(b):
        xs = x[bi]
        pos = positions[bi]
        h = _rmsnorm(xs, norm_mix[0])
        for i in range(depth):
            mixer, j = i % 4, i // 4
            if mixer == 0:
                p = _proj(h, hgrn_w_in[j].astype(BF16), PROJ_TN)
                a = _hgrn_scan(p, lower_bounds[i], hgrn_g_norm[j])
                w_out = hgrn_w_out[j]
            elif mixer == 1:
                main = 2 * GLA_KEY_DIM + 2 * GLA_VALUE_DIM
                w_low = jnp.pad(gla_w_in[j][:, main:], ((0, 0), (0, LANE - GLA_GATE_RANK)))
                w_in = jnp.concatenate([gla_w_in[j][:, :main], w_low], axis=1).astype(BF16)
                p = _proj(h, w_in, w_in.shape[1] // 5)
                w_up = jnp.pad(gla_w_gk_up[j], ((0, LANE - GLA_GATE_RANK), (0, 0))).astype(BF16)
                a = _gla_scan(p, w_up, gla_b_gk[j], gla_g_norm[j])
                w_out = gla_w_out[j]
            elif mixer == 2:
                p = _proj(h, ret_w_in[j].astype(BF16), PROJ_TN)
                a = _ret_scan(p, pos)
                w_out = ret_w_out[j]
            else:
                q, k, v = _mla_prep(h, pos, mla_w_in[j], mla_g_q_lora[j], mla_g_kv_lora[j],
                                    mla_w_uq[j], mla_w_ukv[j], mla_g_qnorm[j], mla_g_knorm[j])
                a = _flash(q, k, v)
                w_out = mla_w_out[j]
            g_next = norm_mix[i + 1] if i + 1 < depth else jnp.ones((d,), F32)
            xs, h = _out_ffn(a, xs, w_out.astype(BF16), norm_ffn[i], ffn_w_gate_up[i].astype(BF16),
                             ffn_w_down[i].astype(BF16), g_next)
        outs.append(xs)
    return outs[0][None] if b == 1 else jnp.stack(outs, axis=0)
```

```python
import functools
import math

import numpy as np
import jax
import jax.numpy as jnp
from jax import lax
from jax.experimental import pallas as pl
from jax.experimental.pallas import tpu as pltpu

F32 = jnp.float32
BF16 = jnp.bfloat16

D_MODEL = 1024
NORM_EPS = 1e-6
ROPE_THETA = 10000.0
HGRN_HEADS, HGRN_DK, HGRN_DV = 8, 128, 128
GLA_HEADS, GLA_DK, GLA_DV = 4, 128, 256
GLA_KEY_DIM, GLA_VALUE_DIM, GLA_GATE_RANK = 512, 1024, 16
GLA_GATE_NORMALIZER = 16.0
RET_HEADS, RET_DK, RET_DV = 8, 128, 256
MLA_HEADS, MLA_Q_LORA, MLA_KV_LORA = 8, 384, 128
MLA_NOPE, MLA_ROPE, MLA_V = 128, 64, 128
MLA_QK = MLA_NOPE + MLA_ROPE
FFN_HIDDEN = 2816

LANE = 128
VMEM_LIMIT_BYTES = 56 * 1024 * 1024

SCAN_CHUNK = 128
PROJ_TM, PROJ_TN = 1024, 1024
FFN_TM = 512
MLA_TM = 256
ATTN_TILE = 512

_NT = (((1,), (1,)), ((), ()))
_TN = (((0,), (0,)), ((), ()))


def _cparams(n_axes):
    return pltpu.CompilerParams(dimension_semantics=("arbitrary",) * n_axes,
                                vmem_limit_bytes=VMEM_LIMIT_BYTES)


def _rms(x, width=None):
    width = x.shape[-1] if width is None else width
    ss = jnp.sum(x * x, axis=-1, keepdims=True)
    return x * lax.rsqrt(ss * (1.0 / width) + NORM_EPS)


def _silu(x):
    return x * jax.nn.sigmoid(x)


def _rmsnorm_kernel(x_ref, g_ref, o_ref):
    o_ref[...] = (_rms(x_ref[...]) * g_ref[...]).astype(o_ref.dtype)


def _rmsnorm(x, gain, tm=1024):
    s, d = x.shape
    tm = min(tm, s)
    return pl.pallas_call(
        _rmsnorm_kernel,
        grid=(s // tm,),
        in_specs=[pl.BlockSpec((tm, d), lambda i: (i, 0)),
                  pl.BlockSpec((1, d), lambda i: (0, 0))],
        out_specs=pl.BlockSpec((tm, d), lambda i: (i, 0)),
        out_shape=jax.ShapeDtypeStruct((s, d), BF16),
        compiler_params=_cparams(1),
        name="rmsnorm",
    )(x, gain.reshape(1, d))


def _proj_kernel(h_ref, w_ref, o_ref):
    acc = jnp.dot(h_ref[...], w_ref[...], preferred_element_type=F32)
    for c in range(acc.shape[1] // LANE):
        o_ref[c] = acc[:, c * LANE:(c + 1) * LANE].astype(o_ref.dtype)


def _proj(h, w, out_dtype):
    s, k = h.shape
    n = w.shape[1]
    w = w.astype(BF16)
    tm = min(PROJ_TM, s)
    tn = min(PROJ_TN, n)
    return pl.pallas_call(
        _proj_kernel,
        grid=(n // tn, s // tm),
        in_specs=[pl.BlockSpec((tm, k), lambda j, i: (i, 0)),
                  pl.BlockSpec((k, tn), lambda j, i: (0, j))],
        out_specs=pl.BlockSpec((tn // LANE, tm, LANE), lambda j, i: (j, i, 0)),
        out_shape=jax.ShapeDtypeStruct((n // LANE, s, LANE), out_dtype),
        compiler_params=_cparams(2),
        name="proj",
    )(h, w)


N_MATMUL_LEVELS = 3


def _scan_tables(chunk):
    n_levels = int(math.log2(chunk))
    t = np.arange(chunk)[:, None]
    r = np.arange(chunk)[None, :]
    blocks = [r <= t]
    level = np.full((chunk, chunk), -1, np.int32)
    level[np.arange(chunk), np.arange(chunk)] = n_levels
    for l in range(n_levels):
        m = 1 << l
        b = (t // (2 * m)) * (2 * m) + m - 1
        if 1 <= l < N_MATMUL_LEVELS:
            blocks.append(np.where(t > b, (r > b) & (r <= t), (r > t) & (r <= b)))
        same = (t // (2 * m)) == (r // (2 * m))
        level[same & (t % (2 * m) >= m) & (r % (2 * m) < m)] = l
    w = np.concatenate(blocks, axis=0).astype(np.float32)
    wcat = np.concatenate([w, w], axis=1)
    return jnp.asarray(wcat, BF16), jnp.asarray(level)


def _gate_sums(g, wcat_ref, cum_ref):
    c = g.shape[0]
    g_hi = g.astype(BF16)
    g_lo = (g - g_hi.astype(F32)).astype(BF16)
    gcat = jnp.concatenate([g_hi, g_lo], axis=0)
    sums = jnp.dot(wcat_ref[...], gcat, preferred_element_type=F32)
    cum_ref[...] = sums[:c]
    return sums[c:]


def _level_exponent(l, g, low, cum_ref, lanes):
    c = g.shape[0]
    if l == 0:
        odd = lax.broadcasted_iota(jnp.int32, g.shape, 0) % 2 == 1
        return jnp.where(odd, g, 0.0)
    if l < N_MATMUL_LEVELS:
        return low[(l - 1) * c:l * c]
    m = 1 << l
    parts = []
    for a in range(0, c, 2 * m):
        edge = cum_ref[a + m - 1:a + m, lanes]
        parts.append(edge - cum_ref[a:a + m, lanes])
        parts.append(cum_ref[a + m:a + 2 * m, lanes] - edge)
    return jnp.concatenate(parts, axis=0)


def _gated_chunk(q, k, vb, g, low, cum_ref, lanes, lv, st_ref):
    c = q.shape[0]
    n_levels = int(math.log2(c))
    scores = jnp.where(lv == n_levels,
                       lax.dot_general(q.astype(BF16), k.astype(BF16), _NT,
                                       preferred_element_type=F32), 0.0)
    for l in range(n_levels):
        e = jnp.exp(_level_exponent(l, g, low, cum_ref, lanes))
        s = lax.dot_general((q * e).astype(BF16), (k * e).astype(BF16), _NT,
                            preferred_element_type=F32)
        scores = jnp.where(lv == l, s, scores)
    cum = cum_ref[:, lanes]
    last = cum_ref[c - 1:c, lanes]
    q_in = (q * jnp.exp(cum)).astype(BF16)
    k_out = (k * jnp.exp(last - cum)).astype(BF16)
    st = st_ref[...]
    o = jnp.dot(scores.astype(BF16), vb, preferred_element_type=F32)
    o = o + lax.dot_general(q_in, st.astype(BF16), _NT, preferred_element_type=F32)
    st_ref[...] = st * jnp.exp(last) + lax.dot_general(vb, k_out, _TN, preferred_element_type=F32)
    return o


def _reset_state(st_ref):
    @pl.when(pl.program_id(0) == 0)
    def _():
        st_ref[...] = jnp.zeros_like(st_ref)


def _cols(p_ref, first, n):
    if n == 1:
        return p_ref[first]
    return jnp.concatenate([p_ref[first + i] for i in range(n)], axis=1)


def _hgrn_kernel(p_ref, f_ref, lb_ref, gn_ref, wcat_ref, lv_ref, o_ref, st_ref, cum_ref):
    _reset_state(st_ref)
    nh = HGRN_HEADS
    lb = lb_ref[...]
    forget = lb + (1.0 - lb) * jax.nn.sigmoid(_cols(f_ref, 0, nh))
    g = jnp.log(forget)
    low = _gate_sums(g, wcat_ref, cum_ref)
    lv = lv_ref[...]
    for hd in range(nh):
        lanes = slice(hd * LANE, (hd + 1) * LANE)
        q = _silu(p_ref[hd].astype(F32)) * (HGRN_DK ** -0.5)
        o = _gated_chunk(q, 1.0 - forget[:, lanes], p_ref[nh + hd], g[:, lanes], low[:, lanes],
                         cum_ref, lanes, lv, st_ref.at[hd])
        o = _rms(o) * gn_ref[...] * _silu(p_ref[2 * nh + hd].astype(F32))
        o_ref[:, lanes] = o.astype(o_ref.dtype)


def _scan_call(body, p, pg, consts, n_heads, dk, dv, name):
    s = p.shape[1]
    c = min(SCAN_CHUNK, s)
    const = lambda a: pl.BlockSpec(a.shape, lambda t: (0,) * a.ndim)
    blocks = lambda a: pl.BlockSpec((a.shape[0], c, LANE), lambda t: (0, t, 0))
    return pl.pallas_call(
        body,
        grid=(s // c,),
        in_specs=[blocks(p), blocks(pg)] + [const(a) for a in consts],
        out_specs=pl.BlockSpec((c, n_heads * dv), lambda t: (t, 0)),
        out_shape=jax.ShapeDtypeStruct((s, n_heads * dv), BF16),
        scratch_shapes=[pltpu.VMEM((n_heads, dv, dk), F32),
                        pltpu.VMEM((c, n_heads * dk), F32)],
        compiler_params=_cparams(1),
        name=name,
    )(p, pg, *consts)


def _hgrn_scan(p, pf, lower_bound, g_norm):
    wcat, level = _scan_tables(min(SCAN_CHUNK, p.shape[1]))
    consts = (lower_bound.reshape(1, -1), g_norm.reshape(1, HGRN_DV), wcat, level)
    return _scan_call(_hgrn_kernel, p, pf, consts, HGRN_HEADS, HGRN_DK, HGRN_DV, "hgrn_scan")


def _log_sigmoid(z):
    return jnp.minimum(z, 0.0) - jnp.log(1.0 + jnp.exp(-jnp.abs(z)))


def _gla_kernel(p_ref, low_ref, wup_ref, b_ref, gn_ref, wcat_ref, lv_ref, o_ref, st_ref, cum_ref):
    _reset_state(st_ref)
    nh = GLA_HEADS
    nv = GLA_DV // LANE
    z = jnp.dot(low_ref[0].astype(BF16), wup_ref[...], preferred_element_type=F32) + b_ref[...]
    g = _log_sigmoid(z) * (1.0 / GLA_GATE_NORMALIZER)
    low = _gate_sums(g, wcat_ref, cum_ref)
    lv = lv_ref[...]
    for hd in range(nh):
        lanes = slice(hd * LANE, (hd + 1) * LANE)
        q = p_ref[hd].astype(F32) * (GLA_DK ** -0.5)
        o = _gated_chunk(q, p_ref[nh + hd].astype(F32), _cols(p_ref, 2 * nh + nv * hd, nv),
                         g[:, lanes], low[:, lanes], cum_ref, lanes, lv, st_ref.at[hd])
        gate = _cols(p_ref, 2 * nh + nv * nh + nv * hd, nv).astype(F32)
        o = _rms(o) * gn_ref[...] * _silu(gate)
        o_ref[:, hd * GLA_DV:(hd + 1) * GLA_DV] = o.astype(o_ref.dtype)


def _gla_scan(p, p_low, w_up, b_gk, g_norm):
    wcat, level = _scan_tables(min(SCAN_CHUNK, p.shape[1]))
    consts = (w_up, b_gk.reshape(1, -1), g_norm.reshape(1, GLA_DV), wcat, level)
    return _scan_call(_gla_kernel, p, p_low, consts, GLA_HEADS, GLA_DK, GLA_DV, "gla_scan")


def _ret_kernel(p_ref, cos_ref, sin_ref, dm_ref, qd_ref, kd_ref, cd_ref, o_ref, st_ref):
    _reset_state(st_ref)
    nh = RET_HEADS
    nv = RET_DV // LANE
    cos = cos_ref[...]
    sin = sin_ref[...]
    half = RET_DK // 2
    for hd in range(nh):
        q = p_ref[hd].astype(F32)
        k = p_ref[nh + hd].astype(F32)
        q = q * cos + pltpu.roll(q, half, 1) * sin
        k = (k * cos + pltpu.roll(k, half, 1) * sin) * (RET_DK ** -0.5)
        vb = _cols(p_ref, 2 * nh + nv * hd, nv)
        scores = lax.dot_general(q.astype(BF16), k.astype(BF16), _NT,
                                 preferred_element_type=F32) * dm_ref[hd]
        st = st_ref[hd]
        o = jnp.dot(scores.astype(BF16), vb, preferred_element_type=F32)
        o = o + lax.dot_general((q * qd_ref[hd]).astype(BF16), st.astype(BF16), _NT,
                                preferred_element_type=F32)
        st_ref[hd] = st * cd_ref[hd] + lax.dot_general(vb, (k * kd_ref[hd]).astype(BF16), _TN,
                                                       preferred_element_type=F32)
        o = _rms(o) * _silu(_cols(p_ref, 2 * nh + nv * nh + nv * hd, nv).astype(F32))
        o_ref[:, hd * RET_DV:(hd + 1) * RET_DV] = o.astype(o_ref.dtype)


def _rope_tables(positions, dim, pad_to):
    half = dim // 2
    inv_freq = 1.0 / (ROPE_THETA ** (jnp.arange(half, dtype=F32) / half))
    ang = positions.astype(F32)[:, None] * inv_freq
    cos, sin = jnp.cos(ang), jnp.sin(ang)
    pad = jnp.zeros((positions.shape[0], pad_to // 2 - half), F32)
    cos_t = jnp.concatenate([cos, pad, cos, pad], axis=1)
    sin_t = jnp.concatenate([-sin, pad, sin, pad], axis=1)
    return cos_t, sin_t


def _ret_scan(p, positions):
    nb, s, _ = p.shape
    c = min(SCAN_CHUNK, s)
    h = RET_HEADS
    cos_t, sin_t = _rope_tables(positions, RET_DK, LANE)
    log_gamma = jnp.log(1.0 - 2.0 ** (-5.0 - jnp.arange(h, dtype=F32)))
    idx = jnp.arange(c, dtype=F32)
    causal = jnp.tril(jnp.ones((c, c), dtype=bool))
    lg = log_gamma[:, None, None]
    dm = jnp.exp(jnp.where(causal, lg * (idx[:, None] - idx[None, :]), -jnp.inf))
    ones = jnp.ones((1, 1, LANE), F32)
    qd = jnp.exp(lg * (idx[None, :, None] + 1.0)) * ones
    kd = jnp.exp(lg * (c - 1.0 - idx[None, :, None])) * ones
    cd = jnp.exp(lg * float(c)) * ones
    tab = pl.BlockSpec((c, LANE), lambda t: (t, 0))
    const = lambda a: pl.BlockSpec(a.shape, lambda t: (0,) * a.ndim)
    return pl.pallas_call(
        _ret_kernel,
        grid=(s // c,),
        in_specs=[pl.BlockSpec((nb, c, LANE), lambda t: (0, t, 0)), tab, tab,
                  const(dm), const(qd), const(kd), const(cd)],
        out_specs=pl.BlockSpec((c, h * RET_DV), lambda t: (t, 0)),
        out_shape=jax.ShapeDtypeStruct((s, h * RET_DV), BF16),
        scratch_shapes=[pltpu.VMEM((h, RET_DV, RET_DK), F32)],
        compiler_params=_cparams(1),
        name="ret_scan",
    )(p, cos_t, sin_t, dm, qd, kd, cd)


MLA_QK_PAD = 2 * LANE


def _mla_prep_kernel(h_ref, win_ref, gq_ref, gkv_ref, wuq_ref, wukv_ref, gqn_ref, gkn_ref,
                     cos_ref, sin_ref, q_ref, k_ref, v_ref):
    nh = MLA_HEADS
    c = jnp.dot(h_ref[...], win_ref[...], preferred_element_type=F32)
    c_q = _rms(c[:, :MLA_Q_LORA]) * gq_ref[...]
    c_kv = _rms(c[:, MLA_Q_LORA:MLA_Q_LORA + MLA_KV_LORA]) * gkv_ref[...]
    k_rope = c[:, MLA_Q_LORA + MLA_KV_LORA:]
    qf = jnp.dot(c_q.astype(BF16), wuq_ref[...], preferred_element_type=F32)
    kvf = jnp.dot(c_kv.astype(BF16), wukv_ref[...], preferred_element_type=F32)
    cos = cos_ref[...]
    sin = sin_ref[...]
    gqn = gqn_ref[...]
    gkn = gkn_ref[...]
    kr_ss = jnp.sum(k_rope * k_rope, axis=-1, keepdims=True)
    scale = MLA_QK ** -0.5 * math.log2(math.e)
    for hd in range(nh):
        q_n = qf[:, hd * LANE:(hd + 1) * LANE]
        q_r = qf[:, (nh + hd) * LANE:(nh + hd + 1) * LANE]
        ss = jnp.sum(q_n * q_n, axis=-1, keepdims=True) + jnp.sum(q_r * q_r, axis=-1, keepdims=True)
        r = lax.rsqrt(ss * (1.0 / MLA_QK) + NORM_EPS)
        q_n = q_n * r * gqn[:, :LANE]
        q_r = q_r * r * gqn[:, LANE:]
        q_r = q_r * cos + pltpu.roll(q_r, LANE // 2, 1) * sin
        q_ref[hd] = (jnp.concatenate([q_n, q_r], axis=1) * scale).astype(q_ref.dtype)
        k_n = kvf[:, hd * LANE:(hd + 1) * LANE]
        ss = jnp.sum(k_n * k_n, axis=-1, keepdims=True) + kr_ss
        r = lax.rsqrt(ss * (1.0 / MLA_QK) + NORM_EPS)
        k_n = k_n * r * gkn[:, :LANE]
        k_r = k_rope * r * gkn[:, LANE:]
        k_r = k_r * cos + pltpu.roll(k_r, LANE // 2, 1) * sin
        k_ref[hd] = jnp.concatenate([k_n, k_r], axis=1).T.astype(k_ref.dtype)
        v_h = kvf[:, (nh + hd) * LANE:(nh + hd + 1) * LANE]
        v_ref[hd] = jnp.concatenate([v_h, jnp.ones_like(v_h)], axis=1).astype(v_ref.dtype)


def _pad_rope_cols(w):
    half = MLA_ROPE // 2
    z = jnp.zeros(w.shape[:-1] + (LANE // 2 - half,), w.dtype)
    return jnp.concatenate([w[..., :half], z, w[..., half:], z], axis=-1)


def _mla_prep(h, positions, w_in, g_q_lora, g_kv_lora, w_uq, w_ukv, g_qnorm, g_knorm):
    s = h.shape[0]
    tm = min(MLA_TM, s)
    nh = MLA_HEADS
    lat = MLA_Q_LORA + MLA_KV_LORA
    w_in_p = jnp.concatenate([w_in[:, :lat], _pad_rope_cols(w_in[:, lat:])], axis=1).astype(BF16)
    wq = w_uq.reshape(MLA_Q_LORA, nh, MLA_QK)
    wq_nope = wq[:, :, :MLA_NOPE].reshape(MLA_Q_LORA, nh * LANE)
    wq_rope = _pad_rope_cols(wq[:, :, MLA_NOPE:]).reshape(MLA_Q_LORA, nh * LANE)
    w_uq_p = jnp.concatenate([wq_nope, wq_rope], axis=1).astype(BF16)
    wkv = w_ukv.reshape(MLA_KV_LORA, nh, MLA_NOPE + MLA_V)
    w_ukv_p = jnp.concatenate([wkv[:, :, :MLA_NOPE].reshape(MLA_KV_LORA, nh * LANE),
                               wkv[:, :, MLA_NOPE:].reshape(MLA_KV_LORA, nh * LANE)],
                              axis=1).astype(BF16)
    pad_gain = lambda g: jnp.concatenate([g[:MLA_NOPE], _pad_rope_cols(g[MLA_NOPE:])]).reshape(1, -1)
    cos_t, sin_t = _rope_tables(positions, MLA_ROPE, LANE)
    const = lambda a: pl.BlockSpec(a.shape, lambda i: (0,) * a.ndim)
    args = (w_in_p, g_q_lora.reshape(1, -1), g_kv_lora.reshape(1, -1), w_uq_p, w_ukv_p,
            pad_gain(g_qnorm), pad_gain(g_knorm))
    tab = pl.BlockSpec((tm, LANE), lambda i: (i, 0))
    return pl.pallas_call(
        _mla_prep_kernel,
        grid=(s // tm,),
        in_specs=[pl.BlockSpec((tm, D_MODEL), lambda i: (i, 0))] + [const(a) for a in args] + [tab, tab],
        out_specs=[pl.BlockSpec((nh, tm, MLA_QK_PAD), lambda i: (0, i, 0)),
                   pl.BlockSpec((nh, MLA_QK_PAD, tm), lambda i: (0, 0, i)),
                   pl.BlockSpec((nh, tm, 2 * MLA_V), lambda i: (0, i, 0))],
        out_shape=[jax.ShapeDtypeStruct((nh, s, MLA_QK_PAD), BF16),
                   jax.ShapeDtypeStruct((nh, MLA_QK_PAD, s), BF16),
                   jax.ShapeDtypeStruct((nh, s, 2 * MLA_V), BF16)],
        compiler_params=_cparams(1),
        name="mla_prep",
    )(h, *args, cos_t, sin_t)


def _flash_kernel(q_ref, k_ref, v_ref, o_ref, s_ref, m_ref, acc_ref, *, tile):
    i = pl.program_id(1)
    m_ref[...] = jnp.full(m_ref.shape, -jnp.inf, F32)
    acc_ref[...] = jnp.zeros(acc_ref.shape, F32)

    def scores(u, j, dst):
        off = pl.multiple_of(j * tile, tile)
        s_ref[dst, u] = jnp.dot(q_ref[0, u * tile:(u + 1) * tile, :], k_ref[0, :, pl.ds(off, tile)],
                                preferred_element_type=F32)

    def accumulate(u, j, src, masked):
        off = pl.multiple_of(j * tile, tile)
        s = s_ref[src, u]
        if masked:
            row = lax.broadcasted_iota(jnp.int32, (tile, tile), 0)
            col = lax.broadcasted_iota(jnp.int32, (tile, tile), 1)
            s = jnp.where(col <= row, s, -jnp.inf)
        m_prev = m_ref[u]
        m_new = jnp.maximum(m_prev, jnp.max(s, axis=-1, keepdims=True))
        p = jnp.exp2(s - m_new).astype(BF16)
        pv = jnp.dot(p, v_ref[0, pl.ds(off, tile), :], preferred_element_type=F32)
        acc_ref[u] = jnp.exp2(m_prev - m_new) * acc_ref[u] + pv
        m_ref[u] = m_new

    for u in range(2):
        scores(u, 0, 0)

    def body(jj, carry):
        for u in range(2):
            scores(u, 2 * jj + 1, 1)
        for u in range(2):
            accumulate(u, 2 * jj, 0, False)
        for u in range(2):
            scores(u, 2 * jj + 2, 0)
        for u in range(2):
            accumulate(u, 2 * jj + 1, 1, False)
        return carry

    lax.fori_loop(0, i, body, 0)
    scores(1, 2 * i + 1, 1)
    accumulate(0, 2 * i, 0, True)
    accumulate(1, 2 * i, 0, False)
    accumulate(1, 2 * i + 1, 1, True)
    for u in range(2):
        acc = acc_ref[u]
        o_ref[u * tile:(u + 1) * tile, :] = (acc[:, :MLA_V] / acc[:, MLA_V:]).astype(o_ref.dtype)


def _flash(q, k, v):
    nh, s, _ = q.shape
    tile = min(ATTN_TILE, s // 2)
    tq = 2 * tile
    return pl.pallas_call(
        functools.partial(_flash_kernel, tile=tile),
        grid=(nh, s // tq),
        in_specs=[pl.BlockSpec((1, tq, MLA_QK_PAD), lambda h, i: (h, i, 0)),
                  pl.BlockSpec((1, MLA_QK_PAD, s), lambda h, i: (h, 0, 0)),
                  pl.BlockSpec((1, s, 2 * MLA_V), lambda h, i: (h, 0, 0))],
        out_specs=pl.BlockSpec((tq, MLA_V), lambda h, i: (i, h)),
        out_shape=jax.ShapeDtypeStruct((s, nh * MLA_V), BF16),
        scratch_shapes=[pltpu.VMEM((2, 2, tile, tile), F32),
                        pltpu.VMEM((2, tile, 1), F32),
                        pltpu.VMEM((2, tile, 2 * MLA_V), F32)],
        compiler_params=_cparams(2),
        name="flash",
    )(q, k, v)


def _out_ffn_kernel(a_ref, x_ref, wo_ref, gf_ref, wgu_ref, wd_ref, gn_ref, xo_ref, *maybe_ho_ref):
    x1 = x_ref[...] + jnp.dot(a_ref[...], wo_ref[...], preferred_element_type=F32)
    h = (_rms(x1) * gf_ref[...]).astype(BF16)
    au = jnp.dot(h, wgu_ref[...], preferred_element_type=F32)
    p = (_silu(au[:, :FFN_HIDDEN]) * au[:, FFN_HIDDEN:]).astype(BF16)
    x2 = x1 + jnp.dot(p, wd_ref[...], preferred_element_type=F32)
    xo_ref[...] = x2
    for ho_ref in maybe_ho_ref:
        ho_ref[...] = (_rms(x2) * gn_ref[...]).astype(ho_ref.dtype)


def _out_ffn(a, x, w_out, g_ffn, w_gate_up, w_down, g_next, emit_next):
    s, d = x.shape
    tm = min(FFN_TM, s)
    row = lambda width: pl.BlockSpec((tm, width), lambda i: (i, 0))
    resident = lambda arr: pl.BlockSpec(arr.shape, lambda i: (0, 0), pipeline_mode=pl.Buffered(1))
    gf = g_ffn.reshape(1, d)
    gn = g_next.reshape(1, d)
    n_out = 2 if emit_next else 1
    return pl.pallas_call(
        _out_ffn_kernel,
        grid=(s // tm,),
        in_specs=[row(a.shape[1]), row(d), resident(w_out), resident(gf), resident(w_gate_up),
                  resident(w_down), resident(gn)],
        out_specs=[row(d), row(d)][:n_out],
        out_shape=[jax.ShapeDtypeStruct((s, d), F32), jax.ShapeDtypeStruct((s, d), BF16)][:n_out],
        compiler_params=_cparams(1),
        name="out_ffn",
    )(a, x, w_out, gf, w_gate_up, w_down, gn)


def kernel(x, positions, norm_mix, norm_ffn, hgrn_w_in, hgrn_g_norm, hgrn_w_out, hgrn_lb_logits,
           gla_w_in, gla_w_gk_up, gla_b_gk, gla_g_norm, gla_w_out, ret_w_in, ret_w_out, mla_w_in,
           mla_g_q_lora, mla_g_kv_lora, mla_w_uq, mla_w_ukv, mla_g_qnorm, mla_g_knorm, mla_w_out,
           ffn_w_gate_up, ffn_w_down):
    b, s, d = x.shape
    depth = norm_mix.shape[0]
    lower_bounds = jnp.cumsum(jax.nn.softmax(hgrn_lb_logits.astype(F32), axis=0), axis=0)
    outs = []
    for bi in range(b):
        xs = x[bi]
        pos = positions[bi]
        h = _rmsnorm(xs, norm_mix[0])
        for i in range(depth):
            mixer, j = i % 4, i // 4
            if mixer == 0:
                w = hgrn_w_in[j]
                p = _proj(h, jnp.concatenate([w[:, :d], w[:, 2 * d:]], axis=1), BF16)
                pf = _proj(h, w[:, d:2 * d], F32)
                a = _hgrn_scan(p, pf, lower_bounds[i], hgrn_g_norm[j])
                w_out = hgrn_w_out[j]
            elif mixer == 1:
                main = 2 * GLA_KEY_DIM + 2 * GLA_VALUE_DIM
                pad = LANE - GLA_GATE_RANK
                p = _proj(h, gla_w_in[j][:, :main], BF16)
                p_low = _proj(h, jnp.pad(gla_w_in[j][:, main:], ((0, 0), (0, pad))), F32)
                w_up = jnp.pad(gla_w_gk_up[j], ((0, pad), (0, 0))).astype(BF16)
                a = _gla_scan(p, p_low, w_up, gla_b_gk[j], gla_g_norm[j])
                w_out = gla_w_out[j]
            elif mixer == 2:
                a = _ret_scan(_proj(h, ret_w_in[j], BF16), pos)
                w_out = ret_w_out[j]
            else:
                q, k, v = _mla_prep(h, pos, mla_w_in[j], mla_g_q_lora[j], mla_g_kv_lora[j],
                                    mla_w_uq[j], mla_w_ukv[j], mla_g_qnorm[j], mla_g_knorm[j])
                a = _flash(q, k, v)
                w_out = mla_w_out[j]
            last = i + 1 == depth
            res = _out_ffn(a, xs, w_out.astype(BF16), norm_ffn[i], ffn_w_gate_up[i].astype(BF16),
                           ffn_w_down[i].astype(BF16), norm_mix[0 if last else i + 1], not last)
            xs = res[0]
            h = None if last else res[1]
        outs.append(xs)
    return outs[0][None] if b == 1 else jnp.stack(outs, axis=0)
```

```python
import functools
import math

import numpy as np
import jax
import jax.numpy as jnp
from jax import lax
from jax.experimental import pallas as pl
from jax.experimental.pallas import tpu as pltpu

F32 = jnp.float32
BF16 = jnp.bfloat16

D_MODEL = 1024
NORM_EPS = 1e-6
ROPE_THETA = 10000.0
HGRN_HEADS, HGRN_DK, HGRN_DV = 8, 128, 128
GLA_HEADS, GLA_DK, GLA_DV = 4, 128, 256
GLA_KEY_DIM, GLA_VALUE_DIM, GLA_GATE_RANK = 512, 1024, 16
GLA_GATE_NORMALIZER = 16.0
RET_HEADS, RET_DK, RET_DV = 8, 128, 256
MLA_HEADS, MLA_Q_LORA, MLA_KV_LORA = 8, 384, 128
MLA_NOPE, MLA_ROPE, MLA_V = 128, 64, 128
MLA_QK = MLA_NOPE + MLA_ROPE
FFN_HIDDEN = 2816

LANE = 128
VMEM_LIMIT_BYTES = 56 * 1024 * 1024

SCAN_CHUNK = 128
PROJ_TM, PROJ_TN = 1024, 1024
FFN_TM = 512
MLA_TM = 256
ATTN_TILE = 512

_NT = (((1,), (1,)), ((), ()))
_TN = (((0,), (0,)), ((), ()))


def _cparams(n_axes):
    return pltpu.CompilerParams(dimension_semantics=("arbitrary",) * n_axes,
                                vmem_limit_bytes=VMEM_LIMIT_BYTES)


def _rms(x, width=None):
    width = x.shape[-1] if width is None else width
    ss = jnp.sum(x * x, axis=-1, keepdims=True)
    return x * lax.rsqrt(ss * (1.0 / width) + NORM_EPS)


def _silu(x):
    return x * jax.nn.sigmoid(x)


def _rmsnorm_kernel(x_ref, g_ref, o_ref):
    o_ref[...] = (_rms(x_ref[...]) * g_ref[...]).astype(o_ref.dtype)


def _rmsnorm(x, gain, tm=1024):
    s, d = x.shape
    tm = min(tm, s)
    return pl.pallas_call(
        _rmsnorm_kernel,
        grid=(s // tm,),
        in_specs=[pl.BlockSpec((tm, d), lambda i: (i, 0)),
                  pl.BlockSpec((1, d), lambda i: (0, 0))],
        out_specs=pl.BlockSpec((tm, d), lambda i: (i, 0)),
        out_shape=jax.ShapeDtypeStruct((s, d), BF16),
        compiler_params=_cparams(1),
        name="rmsnorm",
    )(x, gain.reshape(1, d))


def _proj_kernel(h_ref, w_ref, o_ref):
    acc = jnp.dot(h_ref[...], w_ref[...], preferred_element_type=F32)
    for c in range(acc.shape[1] // LANE):
        o_ref[c] = acc[:, c * LANE:(c + 1) * LANE].astype(o_ref.dtype)


def _proj(h, w, out_dtype):
    s, k = h.shape
    n = w.shape[1]
    w = w.astype(BF16)
    tm = min(PROJ_TM, s)
    tn = min(PROJ_TN, n)
    return pl.pallas_call(
        _proj_kernel,
        grid=(n // tn, s // tm),
        in_specs=[pl.BlockSpec((tm, k), lambda j, i: (i, 0)),
                  pl.BlockSpec((k, tn), lambda j, i: (0, j))],
        out_specs=pl.BlockSpec((tn // LANE, tm, LANE), lambda j, i: (j, i, 0)),
        out_shape=jax.ShapeDtypeStruct((n // LANE, s, LANE), out_dtype),
        compiler_params=_cparams(2),
        name="proj",
    )(h, w)


N_MATMUL_LEVELS = 3


def _scan_tables(chunk):
    n_levels = int(math.log2(chunk))
    t = np.arange(chunk)[:, None]
    r = np.arange(chunk)[None, :]
    blocks = [r <= t]
    level = np.full((chunk, chunk), -1, np.int32)
    level[np.arange(chunk), np.arange(chunk)] = n_levels
    for l in range(n_levels):
        m = 1 << l
        b = (t // (2 * m)) * (2 * m) + m - 1
        if 1 <= l < N_MATMUL_LEVELS:
            blocks.append(np.where(t > b, (r > b) & (r <= t), (r > t) & (r <= b)))
        same = (t // (2 * m)) == (r // (2 * m))
        level[same & (t % (2 * m) >= m) & (r % (2 * m) < m)] = l
    w = np.concatenate(blocks, axis=0).astype(np.float32)
    wcat = np.concatenate([w, w], axis=1)
    return jnp.asarray(wcat, BF16), jnp.asarray(level)


def _gate_sums(g, wcat_ref, cum_ref):
    c = g.shape[0]
    g_hi = g.astype(BF16)
    g_lo = (g - g_hi.astype(F32)).astype(BF16)
    gcat = jnp.concatenate([g_hi, g_lo], axis=0)
    sums = jnp.dot(wcat_ref[...], gcat, preferred_element_type=F32)
    cum_ref[...] = sums[:c]
    return sums[c:]


def _level_exponent(l, g, low, cum_ref, lanes):
    c = g.shape[0]
    if l == 0:
        odd = lax.broadcasted_iota(jnp.int32, g.shape, 0) % 2 == 1
        return jnp.where(odd, g, 0.0)
    if l < N_MATMUL_LEVELS:
        return low[(l - 1) * c:l * c]
    m = 1 << l
    parts = []
    for a in range(0, c, 2 * m):
        edge = cum_ref[a + m - 1:a + m, lanes]
        parts.append(edge - cum_ref[a:a + m, lanes])
        parts.append(cum_ref[a + m:a + 2 * m, lanes] - edge)
    return jnp.concatenate(parts, axis=0)


def _gated_chunk(q, k, vb, g, low, cum_ref, lanes, lv, st_ref):
    c = q.shape[0]
    n_levels = int(math.log2(c))
    scores = jnp.where(lv == n_levels,
                       lax.dot_general(q.astype(BF16), k.astype(BF16), _NT,
                                       preferred_element_type=F32), 0.0)
    for l in range(n_levels):
        e = jnp.exp(_level_exponent(l, g, low, cum_ref, lanes))
        s = jnp.dot((q * e).astype(BF16), (k * e).T.astype(BF16), preferred_element_type=F32)
        scores = jnp.where(lv == l, s, scores)
    cum = cum_ref[:, lanes]
    last = cum_ref[c - 1:c, lanes]
    q_in = (q * jnp.exp(cum)).astype(BF16)
    k_out = (k * jnp.exp(last - cum)).astype(BF16)
    st = st_ref[...]
    o = jnp.dot(scores.astype(BF16), vb, preferred_element_type=F32)
    o = o + lax.dot_general(q_in, st.astype(BF16), _NT, preferred_element_type=F32)
    st_ref[...] = st * jnp.exp(last) + lax.dot_general(vb, k_out, _TN, preferred_element_type=F32)
    return o


def _reset_state(st_ref):
    @pl.when(pl.program_id(0) == 0)
    def _():
        st_ref[...] = jnp.zeros_like(st_ref)


def _cols(p_ref, first, n):
    if n == 1:
        return p_ref[first]
    return jnp.concatenate([p_ref[first + i] for i in range(n)], axis=1)


def _hgrn_kernel(p_ref, f_ref, lb_ref, gn_ref, wcat_ref, lv_ref, o_ref, st_ref, cum_ref):
    _reset_state(st_ref)
    nh = HGRN_HEADS
    lb = lb_ref[...]
    forget = lb + (1.0 - lb) * jax.nn.sigmoid(_cols(f_ref, 0, nh))
    g = jnp.log(forget)
    low = _gate_sums(g, wcat_ref, cum_ref)
    lv = lv_ref[...]
    for hd in range(nh):
        lanes = slice(hd * LANE, (hd + 1) * LANE)
        q = _silu(p_ref[hd].astype(F32)) * (HGRN_DK ** -0.5)
        o = _gated_chunk(q, 1.0 - forget[:, lanes], p_ref[nh + hd], g[:, lanes], low[:, lanes],
                         cum_ref, lanes, lv, st_ref.at[hd])
        o = _rms(o) * gn_ref[...] * _silu(p_ref[2 * nh + hd].astype(F32))
        o_ref[:, lanes] = o.astype(o_ref.dtype)


def _scan_call(body, p, pg, consts, n_heads, dk, dv, name):
    s = p.shape[1]
    c = min(SCAN_CHUNK, s)
    const = lambda a: pl.BlockSpec(a.shape, lambda t: (0,) * a.ndim)
    blocks = lambda a: pl.BlockSpec((a.shape[0], c, LANE), lambda t: (0, t, 0))
    return pl.pallas_call(
        body,
        grid=(s // c,),
        in_specs=[blocks(p), blocks(pg)] + [const(a) for a in consts],
        out_specs=pl.BlockSpec((c, n_heads * dv), lambda t: (t, 0)),
        out_shape=jax.ShapeDtypeStruct((s, n_heads * dv), BF16),
        scratch_shapes=[pltpu.VMEM((n_heads, dv, dk), F32),
                        pltpu.VMEM((c, n_heads * dk), F32)],
        compiler_params=_cparams(1),
        name=name,
    )(p, pg, *consts)


def _hgrn_scan(p, pf, lower_bound, g_norm):
    wcat, level = _scan_tables(min(SCAN_CHUNK, p.shape[1]))
    consts = (lower_bound.reshape(1, -1), g_norm.reshape(1, HGRN_DV), wcat, level)
    return _scan_call(_hgrn_kernel, p, pf, consts, HGRN_HEADS, HGRN_DK, HGRN_DV, "hgrn_scan")


def _log_sigmoid(z):
    return jnp.minimum(z, 0.0) - jnp.log(1.0 + jnp.exp(-jnp.abs(z)))


def _gla_kernel(p_ref, low_ref, wup_ref, b_ref, gn_ref, wcat_ref, lv_ref, o_ref, st_ref, cum_ref):
    _reset_state(st_ref)
    nh = GLA_HEADS
    nv = GLA_DV // LANE
    z = jnp.dot(low_ref[0].astype(BF16), wup_ref[...], preferred_element_type=F32) + b_ref[...]
    g = _log_sigmoid(z) * (1.0 / GLA_GATE_NORMALIZER)
    low = _gate_sums(g, wcat_ref, cum_ref)
    lv = lv_ref[...]
    for hd in range(nh):
        lanes = slice(hd * LANE, (hd + 1) * LANE)
        q = p_ref[hd].astype(F32) * (GLA_DK ** -0.5)
        o = _gated_chunk(q, p_ref[nh + hd].astype(F32), _cols(p_ref, 2 * nh + nv * hd, nv),
                         g[:, lanes], low[:, lanes], cum_ref, lanes, lv, st_ref.at[hd])
        gate = _cols(p_ref, 2 * nh + nv * nh + nv * hd, nv).astype(F32)
        o = _rms(o) * gn_ref[...] * _silu(gate)
        o_ref[:, hd * GLA_DV:(hd + 1) * GLA_DV] = o.astype(o_ref.dtype)


def _gla_scan(p, p_low, w_up, b_gk, g_norm):
    wcat, level = _scan_tables(min(SCAN_CHUNK, p.shape[1]))
    consts = (w_up, b_gk.reshape(1, -1), g_norm.reshape(1, GLA_DV), wcat, level)
    return _scan_call(_gla_kernel, p, p_low, consts, GLA_HEADS, GLA_DK, GLA_DV, "gla_scan")


def _ret_kernel(p_ref, cos_ref, sin_ref, dm_ref, qd_ref, kd_ref, cd_ref, o_ref, st_ref):
    _reset_state(st_ref)
    nh = RET_HEADS
    nv = RET_DV // LANE
    cos = cos_ref[...]
    sin = sin_ref[...]
    half = RET_DK // 2
    for hd in range(nh):
        q = p_ref[hd].astype(F32)
        k = p_ref[nh + hd].astype(F32)
        q = q * cos + pltpu.roll(q, half, 1) * sin
        k = (k * cos + pltpu.roll(k, half, 1) * sin) * (RET_DK ** -0.5)
        vb = _cols(p_ref, 2 * nh + nv * hd, nv)
        scores = lax.dot_general(q.astype(BF16), k.astype(BF16), _NT,
                                 preferred_element_type=F32) * dm_ref[hd]
        st = st_ref[hd]
        o = jnp.dot(scores.astype(BF16), vb, preferred_element_type=F32)
        o = o + lax.dot_general((q * qd_ref[hd]).astype(BF16), st.astype(BF16), _NT,
                                preferred_element_type=F32)
        st_ref[hd] = st * cd_ref[hd] + lax.dot_general(vb, (k * kd_ref[hd]).astype(BF16), _TN,
                                                       preferred_element_type=F32)
        o = _rms(o) * _silu(_cols(p_ref, 2 * nh + nv * nh + nv * hd, nv).astype(F32))
        o_ref[:, hd * RET_DV:(hd + 1) * RET_DV] = o.astype(o_ref.dtype)


def _rope_angles(positions, dim):
    half = dim // 2
    inv_freq = 1.0 / (ROPE_THETA ** (jnp.arange(half, dtype=F32) / half))
    ang = positions.astype(F32)[:, None] * inv_freq
    return jnp.cos(ang), jnp.sin(ang)


def _rope_tables(cos, sin, pad_to):
    pad = jnp.zeros((cos.shape[0], pad_to // 2 - cos.shape[1]), F32)
    cos_t = jnp.concatenate([cos, pad, cos, pad], axis=1)
    sin_t = jnp.concatenate([-sin, pad, sin, pad], axis=1)
    return cos_t, sin_t


def _ret_scan(p, cos, sin):
    nb, s, _ = p.shape
    c = min(SCAN_CHUNK, s)
    h = RET_HEADS
    cos_t, sin_t = _rope_tables(cos, sin, LANE)
    log_gamma = jnp.log(1.0 - 2.0 ** (-5.0 - jnp.arange(h, dtype=F32)))
    idx = jnp.arange(c, dtype=F32)
    causal = jnp.tril(jnp.ones((c, c), dtype=bool))
    lg = log_gamma[:, None, None]
    dm = jnp.exp(jnp.where(causal, lg * (idx[:, None] - idx[None, :]), -jnp.inf))
    ones = jnp.ones((1, 1, LANE), F32)
    qd = jnp.exp(lg * (idx[None, :, None] + 1.0)) * ones
    kd = jnp.exp(lg * (c - 1.0 - idx[None, :, None])) * ones
    cd = jnp.exp(lg * float(c)) * ones
    tab = pl.BlockSpec((c, LANE), lambda t: (t, 0))
    const = lambda a: pl.BlockSpec(a.shape, lambda t: (0,) * a.ndim)
    return pl.pallas_call(
        _ret_kernel,
        grid=(s // c,),
        in_specs=[pl.BlockSpec((nb, c, LANE), lambda t: (0, t, 0)), tab, tab,
                  const(dm), const(qd), const(kd), const(cd)],
        out_specs=pl.BlockSpec((c, h * RET_DV), lambda t: (t, 0)),
        out_shape=jax.ShapeDtypeStruct((s, h * RET_DV), BF16),
        scratch_shapes=[pltpu.VMEM((h, RET_DV, RET_DK), F32)],
        compiler_params=_cparams(1),
        name="ret_scan",
    )(p, cos_t, sin_t, dm, qd, kd, cd)


MLA_QK_PAD = 2 * LANE


def _mla_prep_kernel(h_ref, win_ref, gq_ref, gkv_ref, wuq_ref, wukt_ref, wuv_ref, gqn_ref, gkn_ref,
                     cos_ref, sin_ref, cos_t_ref, sin_t_ref, q_ref, k_ref, v_ref):
    nh = MLA_HEADS
    c = jnp.dot(h_ref[...], win_ref[...], preferred_element_type=F32)
    c_q = _rms(c[:, :MLA_Q_LORA]) * gq_ref[...]
    c_kv = (_rms(c[:, MLA_Q_LORA:MLA_Q_LORA + MLA_KV_LORA]) * gkv_ref[...]).astype(BF16)
    qf = jnp.dot(c_q.astype(BF16), wuq_ref[...], preferred_element_type=F32)
    vf = jnp.dot(c_kv, wuv_ref[...], preferred_element_type=F32)
    knt = lax.dot_general(wukt_ref[...], c_kv, _NT, preferred_element_type=F32)
    krt = c[:, MLA_Q_LORA + MLA_KV_LORA:].T
    kr_ss = jnp.sum(krt * krt, axis=0, keepdims=True)
    cos = cos_ref[...]
    sin = sin_ref[...]
    cos_t = cos_t_ref[...]
    sin_t = sin_t_ref[...]
    gqn = gqn_ref[...]
    gkn = gkn_ref[...]
    half = LANE // 2
    scale = MLA_QK ** -0.5 * math.log2(math.e)
    for hd in range(nh):
        q_n = qf[:, hd * LANE:(hd + 1) * LANE]
        q_r = qf[:, (nh + hd) * LANE:(nh + hd + 1) * LANE]
        ss = jnp.sum(q_n * q_n + q_r * q_r, axis=-1, keepdims=True)
        r = lax.rsqrt(ss * (1.0 / MLA_QK) + NORM_EPS)
        q_n = q_n * r * gqn[:, :LANE]
        q_r = q_r * r * gqn[:, LANE:]
        q_r = q_r * cos + pltpu.roll(q_r, half, 1) * sin
        q_ref[hd] = (jnp.concatenate([q_n, q_r], axis=1) * scale).astype(q_ref.dtype)
        k_n = knt[hd * LANE:(hd + 1) * LANE]
        ss = jnp.sum(k_n * k_n, axis=0, keepdims=True) + kr_ss
        r = lax.rsqrt(ss * (1.0 / MLA_QK) + NORM_EPS)
        k_n = k_n * r * gkn[:LANE]
        k_r = krt * r * gkn[LANE:]
        k_r = k_r * cos_t + jnp.concatenate([k_r[half:], k_r[:half]], axis=0) * sin_t
        k_ref[hd] = jnp.concatenate([k_n, k_r], axis=0).astype(k_ref.dtype)
        v_h = vf[:, hd * LANE:(hd + 1) * LANE]
        v_ref[hd] = jnp.concatenate([v_h, jnp.ones_like(v_h)], axis=1).astype(v_ref.dtype)


def _pad_rope_cols(w):
    half = MLA_ROPE // 2
    z = jnp.zeros(w.shape[:-1] + (LANE // 2 - half,), w.dtype)
    return jnp.concatenate([w[..., :half], z, w[..., half:], z], axis=-1)


def _mla_prep(h, cos, sin, w_in, g_q_lora, g_kv_lora, w_uq, w_ukv, g_qnorm, g_knorm):
    s = h.shape[0]
    tm = min(MLA_TM, s)
    nh = MLA_HEADS
    lat = MLA_Q_LORA + MLA_KV_LORA
    w_in_p = jnp.concatenate([w_in[:, :lat], _pad_rope_cols(w_in[:, lat:])], axis=1).astype(BF16)
    wq = w_uq.reshape(MLA_Q_LORA, nh, MLA_QK)
    wq_nope = wq[:, :, :MLA_NOPE].reshape(MLA_Q_LORA, nh * LANE)
    wq_rope = _pad_rope_cols(wq[:, :, MLA_NOPE:]).reshape(MLA_Q_LORA, nh * LANE)
    w_uq_p = jnp.concatenate([wq_nope, wq_rope], axis=1).astype(BF16)
    wkv = w_ukv.reshape(MLA_KV_LORA, nh, MLA_NOPE + MLA_V)
    w_uk_t = wkv[:, :, :MLA_NOPE].reshape(MLA_KV_LORA, nh * LANE).T.astype(BF16)
    w_uv = wkv[:, :, MLA_NOPE:].reshape(MLA_KV_LORA, nh * LANE).astype(BF16)
    pad_gain = lambda g: jnp.concatenate([g[:MLA_NOPE], _pad_rope_cols(g[MLA_NOPE:])])
    cos_t, sin_t = _rope_tables(cos, sin, LANE)
    const = lambda a: pl.BlockSpec(a.shape, lambda i: (0,) * a.ndim)
    args = (w_in_p, g_q_lora.reshape(1, -1), g_kv_lora.reshape(1, -1), w_uq_p, w_uk_t, w_uv,
            pad_gain(g_qnorm).reshape(1, -1), pad_gain(g_knorm).reshape(-1, 1))
    tab = pl.BlockSpec((tm, LANE), lambda i: (i, 0))
    tab_t = pl.BlockSpec((LANE, tm), lambda i: (0, i))
    return pl.pallas_call(
        _mla_prep_kernel,
        grid=(s // tm,),
        in_specs=([pl.BlockSpec((tm, D_MODEL), lambda i: (i, 0))] + [const(a) for a in args]
                  + [tab, tab, tab_t, tab_t]),
        out_specs=[pl.BlockSpec((nh, tm, MLA_QK_PAD), lambda i: (0, i, 0)),
                   pl.BlockSpec((nh, MLA_QK_PAD, tm), lambda i: (0, 0, i)),
                   pl.BlockSpec((nh, tm, 2 * MLA_V), lambda i: (0, i, 0))],
        out_shape=[jax.ShapeDtypeStruct((nh, s, MLA_QK_PAD), BF16),
                   jax.ShapeDtypeStruct((nh, MLA_QK_PAD, s), BF16),
                   jax.ShapeDtypeStruct((nh, s, 2 * MLA_V), BF16)],
        compiler_params=_cparams(1),
        name="mla_prep",
    )(h, *args, cos_t, sin_t, cos_t.T, sin_t.T)


def _flash_kernel(q_ref, k_ref, v_ref, o_ref, s_ref, m_ref, acc_ref, *, tile):
    i = pl.program_id(1)
    m_ref[...] = jnp.full(m_ref.shape, -jnp.inf, F32)
    acc_ref[...] = jnp.zeros(acc_ref.shape, F32)

    def scores(u, j, dst):
        off = pl.multiple_of(j * tile, tile)
        s_ref[dst, u] = jnp.dot(q_ref[0, u * tile:(u + 1) * tile, :], k_ref[0, :, pl.ds(off, tile)],
                                preferred_element_type=F32)

    def accumulate(u, j, src, masked):
        off = pl.multiple_of(j * tile, tile)
        s = s_ref[src, u]
        if masked:
            row = lax.broadcasted_iota(jnp.int32, (tile, tile), 0)
            col = lax.broadcasted_iota(jnp.int32, (tile, tile), 1)
            s = jnp.where(col <= row, s, -jnp.inf)
        m_prev = m_ref[u]
        m_new = jnp.maximum(m_prev, jnp.max(s, axis=-1, keepdims=True))
        p = jnp.exp2(s - m_new).astype(BF16)
        pv = jnp.dot(p, v_ref[0, pl.ds(off, tile), :], preferred_element_type=F32)
        acc_ref[u] = jnp.exp2(m_prev - m_new) * acc_ref[u] + pv
        m_ref[u] = m_new

    for u in range(2):
        scores(u, 0, 0)

    def body(jj, carry):
        for u in range(2):
            scores(u, 2 * jj + 1, 1)
        for u in range(2):
            accumulate(u, 2 * jj, 0, False)
        for u in range(2):
            scores(u, 2 * jj + 2, 0)
        for u in range(2):
            accumulate(u, 2 * jj + 1, 1, False)
        return carry

    lax.fori_loop(0, i, body, 0)
    scores(1, 2 * i + 1, 1)
    accumulate(0, 2 * i, 0, True)
    accumulate(1, 2 * i, 0, False)
    accumulate(1, 2 * i + 1, 1, True)
    for u in range(2):
        acc = acc_ref[u]
        o_ref[u * tile:(u + 1) * tile, :] = (acc[:, :MLA_V] / acc[:, MLA_V:]).astype(o_ref.dtype)


def _flash(q, k, v):
    nh, s, _ = q.shape
    tile = min(ATTN_TILE, s // 2)
    tq = 2 * tile
    return pl.pallas_call(
        functools.partial(_flash_kernel, tile=tile),
        grid=(nh, s // tq),
        in_specs=[pl.BlockSpec((1, tq, MLA_QK_PAD), lambda h, i: (h, i, 0)),
                  pl.BlockSpec((1, MLA_QK_PAD, s), lambda h, i: (h, 0, 0)),
                  pl.BlockSpec((1, s, 2 * MLA_V), lambda h, i: (h, 0, 0))],
        out_specs=pl.BlockSpec((tq, MLA_V), lambda h, i: (i, h)),
        out_shape=jax.ShapeDtypeStruct((s, nh * MLA_V), BF16),
        scratch_shapes=[pltpu.VMEM((2, 2, tile, tile), F32),
                        pltpu.VMEM((2, tile, 1), F32),
                        pltpu.VMEM((2, tile, 2 * MLA_V), F32)],
        compiler_params=_cparams(2),
        name="flash",
    )(q, k, v)


def _out_ffn_kernel(a_ref, x_ref, wo_ref, gf_ref, wgu_ref, wd_ref, gn_ref, xo_ref, *maybe_ho_ref):
    x1 = x_ref[...] + jnp.dot(a_ref[...], wo_ref[...], preferred_element_type=F32)
    h = (_rms(x1) * gf_ref[...]).astype(BF16)
    au = jnp.dot(h, wgu_ref[...], preferred_element_type=F32)
    p = (_silu(au[:, :FFN_HIDDEN]) * au[:, FFN_HIDDEN:]).astype(BF16)
    x2 = x1 + jnp.dot(p, wd_ref[...], preferred_element_type=F32)
    xo_ref[...] = x2
    for ho_ref in maybe_ho_ref:
        ho_ref[...] = (_rms(x2) * gn_ref[...]).astype(ho_ref.dtype)


def _out_ffn(a, x, w_out, g_ffn, w_gate_up, w_down, g_next, emit_next):
    s, d = x.shape
    tm = min(FFN_TM, s)
    row = lambda width: pl.BlockSpec((tm, width), lambda i: (i, 0))
    resident = lambda arr: pl.BlockSpec(arr.shape, lambda i: (0, 0), pipeline_mode=pl.Buffered(1))
    gf = g_ffn.reshape(1, d)
    gn = g_next.reshape(1, d)
    n_out = 2 if emit_next else 1
    return pl.pallas_call(
        _out_ffn_kernel,
        grid=(s // tm,),
        in_specs=[row(a.shape[1]), row(d), resident(w_out), resident(gf), resident(w_gate_up),
                  resident(w_down), resident(gn)],
        out_specs=[row(d), row(d)][:n_out],
        out_shape=[jax.ShapeDtypeStruct((s, d), F32), jax.ShapeDtypeStruct((s, d), BF16)][:n_out],
        compiler_params=_cparams(1),
        name="out_ffn",
    )(a, x, w_out, gf, w_gate_up, w_down, gn)


def kernel(x, positions, norm_mix, norm_ffn, hgrn_w_in, hgrn_g_norm, hgrn_w_out, hgrn_lb_logits,
           gla_w_in, gla_w_gk_up, gla_b_gk, gla_g_norm, gla_w_out, ret_w_in, ret_w_out, mla_w_in,
           mla_g_q_lora, mla_g_kv_lora, mla_w_uq, mla_w_ukv, mla_g_qnorm, mla_g_knorm, mla_w_out,
           ffn_w_gate_up, ffn_w_down):
    b, s, d = x.shape
    depth = norm_mix.shape[0]
    lower_bounds = jnp.cumsum(jax.nn.softmax(hgrn_lb_logits.astype(F32), axis=0), axis=0)
    outs = []
    for bi in range(b):
        xs = x[bi]
        pos = positions[bi]
        cos, sin = _rope_angles(pos, RET_DK)
        step = RET_DK // MLA_ROPE
        h = _rmsnorm(xs, norm_mix[0])
        for i in range(depth):
            mixer, j = i % 4, i // 4
            if mixer == 0:
                w = hgrn_w_in[j]
                p = _proj(h, jnp.concatenate([w[:, :d], w[:, 2 * d:]], axis=1), BF16)
                pf = _proj(h, w[:, d:2 * d], F32)
                a = _hgrn_scan(p, pf, lower_bounds[i], hgrn_g_norm[j])
                w_out = hgrn_w_out[j]
            elif mixer == 1:
                main = 2 * GLA_KEY_DIM + 2 * GLA_VALUE_DIM
                pad = LANE - GLA_GATE_RANK
                p = _proj(h, gla_w_in[j][:, :main], BF16)
                p_low = _proj(h, jnp.pad(gla_w_in[j][:, main:], ((0, 0), (0, pad))), F32)
                w_up = jnp.pad(gla_w_gk_up[j], ((0, pad), (0, 0))).astype(BF16)
                a = _gla_scan(p, p_low, w_up, gla_b_gk[j], gla_g_norm[j])
                w_out = gla_w_out[j]
            elif mixer == 2:
                a = _ret_scan(_proj(h, ret_w_in[j], BF16), cos, sin)
                w_out = ret_w_out[j]
            else:
                q, k, v = _mla_prep(h, cos[:, ::step], sin[:, ::step], mla_w_in[j], mla_g_q_lora[j], mla_g_kv_lora[j],
                                    mla_w_uq[j], mla_w_ukv[j], mla_g_qnorm[j], mla_g_knorm[j])
                a = _flash(q, k, v)
                w_out = mla_w_out[j]
            last = i + 1 == depth
            res = _out_ffn(a, xs, w_out.astype(BF16), norm_ffn[i], ffn_w_gate_up[i].astype(BF16),
                           ffn_w_down[i].astype(BF16), norm_mix[0 if last else i + 1], not last)
            xs = res[0]
            h = None if last else res[1]
        outs.append(xs)
    return outs[0][None] if b == 1 else jnp.stack(outs, axis=0)
```

```python
import functools
import math

import numpy as np
import jax
import jax.numpy as jnp
from jax import lax
from jax.experimental import pallas as pl
from jax.experimental.pallas import tpu as pltpu

F32 = jnp.float32
BF16 = jnp.bfloat16

D_MODEL = 1024
NORM_EPS = 1e-6
ROPE_THETA = 10000.0
HGRN_HEADS, HGRN_DK, HGRN_DV = 8, 128, 128
GLA_HEADS, GLA_DK, GLA_DV = 4, 128, 256
GLA_KEY_DIM, GLA_VALUE_DIM, GLA_GATE_RANK = 512, 1024, 16
GLA_GATE_NORMALIZER = 16.0
RET_HEADS, RET_DK, RET_DV = 8, 128, 256
MLA_HEADS, MLA_Q_LORA, MLA_KV_LORA = 8, 384, 128
MLA_NOPE, MLA_ROPE, MLA_V = 128, 64, 128
MLA_QK = MLA_NOPE + MLA_ROPE
FFN_HIDDEN = 2816

LANE = 128
VMEM_LIMIT_BYTES = 56 * 1024 * 1024

SCAN_CHUNK = 128
PROJ_TM, PROJ_TN = 1024, 1024
FFN_TM = 512
MLA_TM = 256
ATTN_TK = 512
ATTN_SUB = 512

_NT = (((1,), (1,)), ((), ()))
_TN = (((0,), (0,)), ((), ()))


def _cparams(n_axes):
    return pltpu.CompilerParams(dimension_semantics=("arbitrary",) * n_axes,
                                vmem_limit_bytes=VMEM_LIMIT_BYTES)


def _rms(x, width=None):
    width = x.shape[-1] if width is None else width
    ss = jnp.sum(x * x, axis=-1, keepdims=True)
    return x * lax.rsqrt(ss * (1.0 / width) + NORM_EPS)


def _silu(x):
    return x * jax.nn.sigmoid(x)


def _rmsnorm_kernel(x_ref, g_ref, o_ref):
    o_ref[...] = (_rms(x_ref[...]) * g_ref[...]).astype(o_ref.dtype)


def _rmsnorm(x, gain, tm=1024):
    s, d = x.shape
    tm = min(tm, s)
    return pl.pallas_call(
        _rmsnorm_kernel,
        grid=(s // tm,),
        in_specs=[pl.BlockSpec((tm, d), lambda i: (i, 0)),
                  pl.BlockSpec((1, d), lambda i: (0, 0))],
        out_specs=pl.BlockSpec((tm, d), lambda i: (i, 0)),
        out_shape=jax.ShapeDtypeStruct((s, d), BF16),
        compiler_params=_cparams(1),
        name="rmsnorm",
    )(x, gain.reshape(1, d))


def _proj_kernel(h_ref, w_ref, o_ref):
    acc = jnp.dot(h_ref[...], w_ref[...].astype(BF16), preferred_element_type=F32)
    for c in range(acc.shape[1] // LANE):
        o_ref[c] = acc[:, c * LANE:(c + 1) * LANE].astype(o_ref.dtype)


def _proj(h, w, n_tiles, col_tile, out_dtype):
    s, k = h.shape
    tm = min(PROJ_TM, s)
    tn = min(PROJ_TN, w.shape[1])
    return pl.pallas_call(
        _proj_kernel,
        grid=(n_tiles, s // tm),
        in_specs=[pl.BlockSpec((tm, k), lambda j, i: (i, 0)),
                  pl.BlockSpec((k, tn), lambda j, i: (0, col_tile(j)))],
        out_specs=pl.BlockSpec((tn // LANE, tm, LANE), lambda j, i: (j, i, 0)),
        out_shape=jax.ShapeDtypeStruct((n_tiles * tn // LANE, s, LANE), out_dtype),
        compiler_params=_cparams(2),
        name="proj",
    )(h, w)


N_MATMUL_LEVELS = 3


def _scan_tables(chunk):
    n_levels = int(math.log2(chunk))
    t = np.arange(chunk)[:, None]
    r = np.arange(chunk)[None, :]
    blocks = [r <= t]
    level = np.full((chunk, chunk), -1, np.int32)
    level[np.arange(chunk), np.arange(chunk)] = n_levels
    for l in range(n_levels):
        m = 1 << l
        b = (t // (2 * m)) * (2 * m) + m - 1
        if 1 <= l < N_MATMUL_LEVELS:
            blocks.append(np.where(t > b, (r > b) & (r <= t), (r > t) & (r <= b)))
        same = (t // (2 * m)) == (r // (2 * m))
        level[same & (t % (2 * m) >= m) & (r % (2 * m) < m)] = l
    w = np.concatenate(blocks, axis=0).astype(np.float32)
    wcat = np.concatenate([w, w], axis=1)
    return jnp.asarray(wcat, BF16), jnp.asarray(level)


def _gate_sums(g, wcat_ref, cum_ref):
    c = g.shape[0]
    g_hi = g.astype(BF16)
    g_lo = (g - g_hi.astype(F32)).astype(BF16)
    gcat = jnp.concatenate([g_hi, g_lo], axis=0)
    sums = jnp.dot(wcat_ref[...], gcat, preferred_element_type=F32)
    cum_ref[...] = sums[:c]
    return sums[c:]


def _level_exponent(l, g, low, cum_ref, lanes):
    c = g.shape[0]
    if l == 0:
        odd = lax.broadcasted_iota(jnp.int32, g.shape, 0) % 2 == 1
        return jnp.where(odd, g, 0.0)
    if l < N_MATMUL_LEVELS:
        return low[(l - 1) * c:l * c]
    m = 1 << l
    parts = []
    for a in range(0, c, 2 * m):
        edge = cum_ref[a + m - 1:a + m, lanes]
        parts.append(edge - cum_ref[a:a + m, lanes])
        parts.append(cum_ref[a + m:a + 2 * m, lanes] - edge)
    return jnp.concatenate(parts, axis=0)


def _gated_chunk(q, k, vb, g, low, cum_ref, lanes, lv, st_ref):
    c = q.shape[0]
    n_levels = int(math.log2(c))
    scores = jnp.where(lv == n_levels,
                       lax.dot_general(q.astype(BF16), k.astype(BF16), _NT,
                                       preferred_element_type=F32), 0.0)
    for l in range(n_levels):
        e = jnp.exp(_level_exponent(l, g, low, cum_ref, lanes))
        s = jnp.dot((q * e).astype(BF16), (k * e).T.astype(BF16), preferred_element_type=F32)
        scores = jnp.where(lv == l, s, scores)
    cum = cum_ref[:, lanes]
    last = cum_ref[c - 1:c, lanes]
    q_in = (q * jnp.exp(cum)).astype(BF16)
    k_out = (k * jnp.exp(last - cum)).astype(BF16)
    st = st_ref[...]
    o = jnp.dot(scores.astype(BF16), vb, preferred_element_type=F32)
    o = o + lax.dot_general(q_in, st.astype(BF16), _NT, preferred_element_type=F32)
    st_ref[...] = st * jnp.exp(last) + lax.dot_general(vb, k_out, _TN, preferred_element_type=F32)
    return o


def _reset_state(st_ref):
    @pl.when(pl.program_id(0) == 0)
    def _():
        st_ref[...] = jnp.zeros_like(st_ref)


def _cols(p_ref, first, n):
    if n == 1:
        return p_ref[first]
    return jnp.concatenate([p_ref[first + i] for i in range(n)], axis=1)


def _hgrn_kernel(p_ref, f_ref, lb_ref, gn_ref, wcat_ref, lv_ref, o_ref, st_ref, cum_ref):
    _reset_state(st_ref)
    nh = HGRN_HEADS
    lb = lb_ref[...]
    forget = lb + (1.0 - lb) * jax.nn.sigmoid(_cols(f_ref, 0, nh))
    g = jnp.log(forget)
    low = _gate_sums(g, wcat_ref, cum_ref)
    lv = lv_ref[...]
    for hd in range(nh):
        lanes = slice(hd * LANE, (hd + 1) * LANE)
        q = _silu(p_ref[hd].astype(F32)) * (HGRN_DK ** -0.5)
        o = _gated_chunk(q, 1.0 - forget[:, lanes], p_ref[nh + hd], g[:, lanes], low[:, lanes],
                         cum_ref, lanes, lv, st_ref.at[hd])
        o = _rms(o) * gn_ref[...] * _silu(p_ref[2 * nh + hd].astype(F32))
        o_ref[:, lanes] = o.astype(o_ref.dtype)


def _scan_call(body, p, pg, consts, n_heads, dk, dv, name):
    s = p.shape[1]
    c = min(SCAN_CHUNK, s)
    const = lambda a: pl.BlockSpec(a.shape, lambda t: (0,) * a.ndim)
    blocks = lambda a: pl.BlockSpec((a.shape[0], c, LANE), lambda t: (0, t, 0))
    return pl.pallas_call(
        body,
        grid=(s // c,),
        in_specs=[blocks(p), blocks(pg)] + [const(a) for a in consts],
        out_specs=pl.BlockSpec((c, n_heads * dv), lambda t: (t, 0)),
        out_shape=jax.ShapeDtypeStruct((s, n_heads * dv), BF16),
        scratch_shapes=[pltpu.VMEM((n_heads, dv, dk), F32),
                        pltpu.VMEM((c, n_heads * dk), F32)],
        compiler_params=_cparams(1),
        name=name,
    )(p, pg, *consts)


def _hgrn_scan(p, pf, lower_bound, g_norm):
    wcat, level = _scan_tables(min(SCAN_CHUNK, p.shape[1]))
    consts = (lower_bound.reshape(1, -1), g_norm.reshape(1, HGRN_DV), wcat, level)
    return _scan_call(_hgrn_kernel, p, pf, consts, HGRN_HEADS, HGRN_DK, HGRN_DV, "hgrn_scan")


def _log_sigmoid(z):
    return jnp.minimum(z, 0.0) - jnp.log(1.0 + jnp.exp(-jnp.abs(z)))


def _gla_kernel(p_ref, low_ref, wup_ref, b_ref, gn_ref, wcat_ref, lv_ref, o_ref, st_ref, cum_ref):
    _reset_state(st_ref)
    nh = GLA_HEADS
    nv = GLA_DV // LANE
    z = jnp.dot(low_ref[0].astype(BF16), wup_ref[...], preferred_element_type=F32) + b_ref[...]
    g = _log_sigmoid(z) * (1.0 / GLA_GATE_NORMALIZER)
    low = _gate_sums(g, wcat_ref, cum_ref)
    lv = lv_ref[...]
    for hd in range(nh):
        lanes = slice(hd * LANE, (hd + 1) * LANE)
        q = p_ref[hd].astype(F32) * (GLA_DK ** -0.5)
        o = _gated_chunk(q, p_ref[nh + hd].astype(F32), _cols(p_ref, 2 * nh + nv * hd, nv),
                         g[:, lanes], low[:, lanes], cum_ref, lanes, lv, st_ref.at[hd])
        gate = _cols(p_ref, 2 * nh + nv * nh + nv * hd, nv).astype(F32)
        o = _rms(o) * gn_ref[...] * _silu(gate)
        o_ref[:, hd * GLA_DV:(hd + 1) * GLA_DV] = o.astype(o_ref.dtype)


def _gla_scan(p, p_low, w_up, b_gk, g_norm):
    wcat, level = _scan_tables(min(SCAN_CHUNK, p.shape[1]))
    consts = (w_up, b_gk.reshape(1, -1), g_norm.reshape(1, GLA_DV), wcat, level)
    return _scan_call(_gla_kernel, p, p_low, consts, GLA_HEADS, GLA_DK, GLA_DV, "gla_scan")


def _ret_kernel(p_ref, cos_ref, sin_ref, dm_ref, qd_ref, kd_ref, cd_ref, o_ref, st_ref):
    _reset_state(st_ref)
    nh = RET_HEADS
    nv = RET_DV // LANE
    cos = cos_ref[...]
    sin = sin_ref[...]
    half = RET_DK // 2
    for hd in range(nh):
        q = p_ref[hd].astype(F32)
        k = p_ref[nh + hd].astype(F32)
        q = q * cos + pltpu.roll(q, half, 1) * sin
        k = (k * cos + pltpu.roll(k, half, 1) * sin) * (RET_DK ** -0.5)
        vb = _cols(p_ref, 2 * nh + nv * hd, nv)
        scores = lax.dot_general(q.astype(BF16), k.astype(BF16), _NT,
                                 preferred_element_type=F32) * dm_ref[hd]
        st = st_ref[hd]
        o = jnp.dot(scores.astype(BF16), vb, preferred_element_type=F32)
        o = o + lax.dot_general((q * qd_ref[hd]).astype(BF16), st.astype(BF16), _NT,
                                preferred_element_type=F32)
        st_ref[hd] = st * cd_ref[hd] + lax.dot_general(vb, (k * kd_ref[hd]).astype(BF16), _TN,
                                                       preferred_element_type=F32)
        o = _rms(o) * _silu(_cols(p_ref, 2 * nh + nv * nh + nv * hd, nv).astype(F32))
        o_ref[:, hd * RET_DV:(hd + 1) * RET_DV] = o.astype(o_ref.dtype)


def _rope_angles(positions, dim):
    half = dim // 2
    inv_freq = 1.0 / (ROPE_THETA ** (jnp.arange(half, dtype=F32) / half))
    ang = positions.astype(F32)[:, None] * inv_freq
    return jnp.cos(ang), jnp.sin(ang)


def _rope_tables(cos, sin, pad_to):
    pad = jnp.zeros((cos.shape[0], pad_to // 2 - cos.shape[1]), F32)
    cos_t = jnp.concatenate([cos, pad, cos, pad], axis=1)
    sin_t = jnp.concatenate([-sin, pad, sin, pad], axis=1)
    return cos_t, sin_t


def _ret_scan(p, cos, sin):
    nb, s, _ = p.shape
    c = min(SCAN_CHUNK, s)
    h = RET_HEADS
    cos_t, sin_t = _rope_tables(cos, sin, LANE)
    log_gamma = jnp.log(1.0 - 2.0 ** (-5.0 - jnp.arange(h, dtype=F32)))
    idx = jnp.arange(c, dtype=F32)
    causal = jnp.tril(jnp.ones((c, c), dtype=bool))
    lg = log_gamma[:, None, None]
    dm = jnp.exp(jnp.where(causal, lg * (idx[:, None] - idx[None, :]), -jnp.inf))
    ones = jnp.ones((1, 1, LANE), F32)
    qd = jnp.exp(lg * (idx[None, :, None] + 1.0)) * ones
    kd = jnp.exp(lg * (c - 1.0 - idx[None, :, None])) * ones
    cd = jnp.exp(lg * float(c)) * ones
    tab = pl.BlockSpec((c, LANE), lambda t: (t, 0))
    const = lambda a: pl.BlockSpec(a.shape, lambda t: (0,) * a.ndim)
    return pl.pallas_call(
        _ret_kernel,
        grid=(s // c,),
        in_specs=[pl.BlockSpec((nb, c, LANE), lambda t: (0, t, 0)), tab, tab,
                  const(dm), const(qd), const(kd), const(cd)],
        out_specs=pl.BlockSpec((c, h * RET_DV), lambda t: (t, 0)),
        out_shape=jax.ShapeDtypeStruct((s, h * RET_DV), BF16),
        scratch_shapes=[pltpu.VMEM((h, RET_DV, RET_DK), F32)],
        compiler_params=_cparams(1),
        name="ret_scan",
    )(p, cos_t, sin_t, dm, qd, kd, cd)


MLA_QK_PAD = 2 * LANE


def _mla_prep_kernel(h_ref, win_ref, gq_ref, gkv_ref, wuq_ref, wukt_ref, wuv_ref, gqn_ref, gkn_ref,
                     cos_ref, sin_ref, cos_t_ref, sin_t_ref, q_ref, k_ref, v_ref):
    nh = MLA_HEADS
    c = jnp.dot(h_ref[...], win_ref[...], preferred_element_type=F32)
    c_q = _rms(c[:, :MLA_Q_LORA]) * gq_ref[...]
    c_kv = (_rms(c[:, MLA_Q_LORA:MLA_Q_LORA + MLA_KV_LORA]) * gkv_ref[...]).astype(BF16)
    qf = jnp.dot(c_q.astype(BF16), wuq_ref[...], preferred_element_type=F32)
    vf = jnp.dot(c_kv, wuv_ref[...], preferred_element_type=F32)
    knt = lax.dot_general(wukt_ref[...], c_kv, _NT, preferred_element_type=F32)
    krt = c[:, MLA_Q_LORA + MLA_KV_LORA:].T
    kr_ss = jnp.sum(krt * krt, axis=0, keepdims=True)
    cos = cos_ref[...]
    sin = sin_ref[...]
    cos_t = cos_t_ref[...]
    sin_t = sin_t_ref[...]
    gqn = gqn_ref[...]
    gkn = gkn_ref[...]
    half = LANE // 2
    scale = MLA_QK ** -0.5 * math.log2(math.e)
    for hd in range(nh):
        q_n = qf[:, hd * LANE:(hd + 1) * LANE]
        q_r = qf[:, (nh + hd) * LANE:(nh + hd + 1) * LANE]
        ss = jnp.sum(q_n * q_n + q_r * q_r, axis=-1, keepdims=True)
        r = lax.rsqrt(ss * (1.0 / MLA_QK) + NORM_EPS)
        q_n = q_n * r * gqn[:, :LANE]
        q_r = q_r * r * gqn[:, LANE:]
        q_r = q_r * cos + pltpu.roll(q_r, half, 1) * sin
        q_ref[hd] = (jnp.concatenate([q_n, q_r], axis=1) * scale).astype(q_ref.dtype)
        k_n = knt[hd * LANE:(hd + 1) * LANE]
        ss = jnp.sum(k_n * k_n, axis=0, keepdims=True) + kr_ss
        r = lax.rsqrt(ss * (1.0 / MLA_QK) + NORM_EPS)
        k_n = k_n * r * gkn[:LANE]
        k_r = krt * r * gkn[LANE:]
        k_r = k_r * cos_t + jnp.concatenate([k_r[half:], k_r[:half]], axis=0) * sin_t
        k_ref[hd] = jnp.concatenate([k_n, k_r], axis=0).astype(k_ref.dtype)
        v_h = vf[:, hd * LANE:(hd + 1) * LANE]
        v_ref[hd] = jnp.concatenate([v_h, jnp.ones_like(v_h)], axis=1).astype(v_ref.dtype)


def _pad_rope_cols(w):
    half = MLA_ROPE // 2
    z = jnp.zeros(w.shape[:-1] + (LANE // 2 - half,), w.dtype)
    return jnp.concatenate([w[..., :half], z, w[..., half:], z], axis=-1)


def _mla_prep(h, cos, sin, w_in, g_q_lora, g_kv_lora, w_uq, w_ukv, g_qnorm, g_knorm):
    s = h.shape[0]
    tm = min(MLA_TM, s)
    nh = MLA_HEADS
    lat = MLA_Q_LORA + MLA_KV_LORA
    w_in_p = jnp.concatenate([w_in[:, :lat], _pad_rope_cols(w_in[:, lat:])], axis=1).astype(BF16)
    wq = w_uq.reshape(MLA_Q_LORA, nh, MLA_QK)
    wq_nope = wq[:, :, :MLA_NOPE].reshape(MLA_Q_LORA, nh * LANE)
    wq_rope = _pad_rope_cols(wq[:, :, MLA_NOPE:]).reshape(MLA_Q_LORA, nh * LANE)
    w_uq_p = jnp.concatenate([wq_nope, wq_rope], axis=1).astype(BF16)
    wkv = w_ukv.reshape(MLA_KV_LORA, nh, MLA_NOPE + MLA_V)
    w_uk_t = wkv[:, :, :MLA_NOPE].reshape(MLA_KV_LORA, nh * LANE).T.astype(BF16)
    w_uv = wkv[:, :, MLA_NOPE:].reshape(MLA_KV_LORA, nh * LANE).astype(BF16)
    pad_gain = lambda g: jnp.concatenate([g[:MLA_NOPE], _pad_rope_cols(g[MLA_NOPE:])])
    cos_t, sin_t = _rope_tables(cos, sin, LANE)
    const = lambda a: pl.BlockSpec(a.shape, lambda i: (0,) * a.ndim)
    args = (w_in_p, g_q_lora.reshape(1, -1), g_kv_lora.reshape(1, -1), w_uq_p, w_uk_t, w_uv,
            pad_gain(g_qnorm).reshape(1, -1), pad_gain(g_knorm).reshape(-1, 1))
    tab = pl.BlockSpec((tm, LANE), lambda i: (i, 0))
    tab_t = pl.BlockSpec((LANE, tm), lambda i: (0, i))
    return pl.pallas_call(
        _mla_prep_kernel,
        grid=(s // tm,),
        in_specs=([pl.BlockSpec((tm, D_MODEL), lambda i: (i, 0))] + [const(a) for a in args]
                  + [tab, tab, tab_t, tab_t]),
        out_specs=[pl.BlockSpec((nh, tm, MLA_QK_PAD), lambda i: (0, i, 0)),
                   pl.BlockSpec((nh, MLA_QK_PAD, tm), lambda i: (0, 0, i)),
                   pl.BlockSpec((nh, tm, 2 * MLA_V), lambda i: (0, i, 0))],
        out_shape=[jax.ShapeDtypeStruct((nh, s, MLA_QK_PAD), BF16),
                   jax.ShapeDtypeStruct((nh, MLA_QK_PAD, s), BF16),
                   jax.ShapeDtypeStruct((nh, s, 2 * MLA_V), BF16)],
        compiler_params=_cparams(1),
        name="mla_prep",
    )(h, *args, cos_t, sin_t, cos_t.T, sin_t.T)


def _flash_kernel(q_ref, qn_ref, k_ref, v_ref, o_ref, s_ref, m_ref, acc_ref, *, sub, tk):
    r = sub // tk
    i = pl.program_id(1)
    m_ref[...] = jnp.full(m_ref.shape, -jnp.inf, F32)
    acc_ref[...] = jnp.zeros(acc_ref.shape, F32)

    def scores(u, j, dst, src_ref=q_ref):
        off = pl.multiple_of(j * tk, tk)
        s_ref[dst, u] = jnp.dot(src_ref[0, u * sub:(u + 1) * sub, :], k_ref[0, :, pl.ds(off, tk)],
                                preferred_element_type=F32)

    def accumulate(u, j, src, shift):
        off = pl.multiple_of(j * tk, tk)
        s = s_ref[src, u]
        if shift is not None:
            row = lax.broadcasted_iota(jnp.int32, (sub, tk), 0)
            col = lax.broadcasted_iota(jnp.int32, (sub, tk), 1)
            s = jnp.where(col - row <= shift, s, -jnp.inf)
        m_prev = m_ref[u]
        m_new = jnp.maximum(m_prev, jnp.max(s, axis=-1, keepdims=True))
        p = jnp.exp2(s - m_new).astype(BF16)
        pv = jnp.dot(p, v_ref[0, pl.ds(off, tk), :], preferred_element_type=F32)
        acc_ref[u] = jnp.exp2(m_prev - m_new) * acc_ref[u] + pv
        m_ref[u] = m_new

    @pl.when(i == 0)
    def _():
        for u in range(2):
            scores(u, 0, 0)

    def body(jj, carry):
        for u in range(2):
            scores(u, 2 * jj + 1, 1)
        for u in range(2):
            accumulate(u, 2 * jj, 0, None)
        for u in range(2):
            scores(u, 2 * jj + 2, 0)
        for u in range(2):
            accumulate(u, 2 * jj + 1, 1, None)
        return carry

    lax.fori_loop(0, r * i, body, 0)
    base = 2 * r * i
    for d in range(2 * r):
        if d + 1 < 2 * r:
            for u in range(2):
                if d + 1 < r * (u + 1):
                    scores(u, base + d + 1, (d + 1) % 2)
        else:
            for u in range(2):
                scores(u, 0, 0, qn_ref)
        for u in range(2):
            if d < r * (u + 1):
                accumulate(u, base + d, d % 2, None if d < r * u else (r * u - d) * tk)
    for u in range(2):
        acc = acc_ref[u]
        o_ref[u * sub:(u + 1) * sub, :] = (acc[:, :MLA_V] / acc[:, MLA_V:]).astype(o_ref.dtype)


def _flash(q, k, v):
    nh, s, _ = q.shape
    sub = min(ATTN_SUB, s // 2)
    tk = min(ATTN_TK, sub)
    tq = 2 * sub
    n_blocks = s // tq
    q_spec = lambda index: pl.BlockSpec((1, tq, MLA_QK_PAD), index)
    return pl.pallas_call(
        functools.partial(_flash_kernel, sub=sub, tk=tk),
        grid=(nh, n_blocks),
        in_specs=[q_spec(lambda h, i: (h, i, 0)),
                  q_spec(lambda h, i: (h, jnp.minimum(i + 1, n_blocks - 1), 0)),
                  pl.BlockSpec((1, MLA_QK_PAD, s), lambda h, i: (h, 0, 0)),
                  pl.BlockSpec((1, s, 2 * MLA_V), lambda h, i: (h, 0, 0))],
        out_specs=pl.BlockSpec((tq, MLA_V), lambda h, i: (i, h)),
        out_shape=jax.ShapeDtypeStruct((s, nh * MLA_V), BF16),
        scratch_shapes=[pltpu.VMEM((2, 2, sub, tk), F32),
                        pltpu.VMEM((2, sub, 1), F32),
                        pltpu.VMEM((2, sub, 2 * MLA_V), F32)],
        compiler_params=_cparams(2),
        name="flash",
    )(q, q, k, v)


def _out_ffn_kernel(a_ref, x_ref, wo_ref, gf_ref, wgu_ref, wd_ref, gn_ref, xo_ref, *maybe_ho_ref):
    x1 = x_ref[...] + jnp.dot(a_ref[...], wo_ref[...], preferred_element_type=F32)
    h = (_rms(x1) * gf_ref[...]).astype(BF16)
    au = jnp.dot(h, wgu_ref[...], preferred_element_type=F32)
    p = (_silu(au[:, :FFN_HIDDEN]) * au[:, FFN_HIDDEN:]).astype(BF16)
    x2 = x1 + jnp.dot(p, wd_ref[...], preferred_element_type=F32)
    xo_ref[...] = x2
    for ho_ref in maybe_ho_ref:
        ho_ref[...] = (_rms(x2) * gn_ref[...]).astype(ho_ref.dtype)


def _out_ffn(a, x, w_out, g_ffn, w_gate_up, w_down, g_next, emit_next):
    s, d = x.shape
    tm = min(FFN_TM, s)
    row = lambda width: pl.BlockSpec((tm, width), lambda i: (i, 0))
    resident = lambda arr: pl.BlockSpec(arr.shape, lambda i: (0, 0), pipeline_mode=pl.Buffered(1))
    gf = g_ffn.reshape(1, d)
    gn = g_next.reshape(1, d)
    n_out = 2 if emit_next else 1
    return pl.pallas_call(
        _out_ffn_kernel,
        grid=(s // tm,),
        in_specs=[row(a.shape[1]), row(d), resident(w_out), resident(gf), resident(w_gate_up),
                  resident(w_down), resident(gn)],
        out_specs=[row(d), row(d)][:n_out],
        out_shape=[jax.ShapeDtypeStruct((s, d), F32), jax.ShapeDtypeStruct((s, d), BF16)][:n_out],
        compiler_params=_cparams(1),
        name="out_ffn",
    )(a, x, w_out, gf, w_gate_up, w_down, gn)


def kernel(x, positions, norm_mix, norm_ffn, hgrn_w_in, hgrn_g_norm, hgrn_w_out, hgrn_lb_logits,
           gla_w_in, gla_w_gk_up, gla_b_gk, gla_g_norm, gla_w_out, ret_w_in, ret_w_out, mla_w_in,
           mla_g_q_lora, mla_g_kv_lora, mla_w_uq, mla_w_ukv, mla_g_qnorm, mla_g_knorm, mla_w_out,
           ffn_w_gate_up, ffn_w_down):
    b, s, d = x.shape
    depth = norm_mix.shape[0]
    lower_bounds = jnp.cumsum(jax.nn.softmax(hgrn_lb_logits.astype(F32), axis=0), axis=0)
    outs = []
    for bi in range(b):
        xs = x[bi]
        pos = positions[bi]
        cos, sin = _rope_angles(pos, RET_DK)
        step = RET_DK // MLA_ROPE
        h = _rmsnorm(xs, norm_mix[0])
        for i in range(depth):
            mixer, j = i % 4, i // 4
            if mixer == 0:
                w = hgrn_w_in[j]
                p = _proj(h, w, 3, lambda t: t + jnp.minimum(t, 1), BF16)
                pf = _proj(h, w, 1, lambda t: t + 1, F32)
                a = _hgrn_scan(p, pf, lower_bounds[i], hgrn_g_norm[j])
                w_out = hgrn_w_out[j]
            elif mixer == 1:
                main = 2 * GLA_KEY_DIM + 2 * GLA_VALUE_DIM
                pad = LANE - GLA_GATE_RANK
                p = _proj(h, gla_w_in[j], main // PROJ_TN, lambda t: t, BF16)
                w_low = jnp.pad(gla_w_in[j][:, main:], ((0, 0), (0, pad)))
                p_low = _proj(h, w_low, 1, lambda t: t, F32)
                w_up = jnp.pad(gla_w_gk_up[j], ((0, pad), (0, 0))).astype(BF16)
                a = _gla_scan(p, p_low, w_up, gla_b_gk[j], gla_g_norm[j])
                w_out = gla_w_out[j]
            elif mixer == 2:
                p = _proj(h, ret_w_in[j], ret_w_in.shape[2] // PROJ_TN, lambda t: t, BF16)
                a = _ret_scan(p, cos, sin)
                w_out = ret_w_out[j]
            else:
                q, k, v = _mla_prep(h, cos[:, ::step], sin[:, ::step], mla_w_in[j], mla_g_q_lora[j], mla_g_kv_lora[j],
                                    mla_w_uq[j], mla_w_ukv[j], mla_g_qnorm[j], mla_g_knorm[j])
                a = _flash(q, k, v)
                w_out = mla_w_out[j]
            last = i + 1 == depth
            res = _out_ffn(a, xs, w_out.astype(BF16), norm_ffn[i], ffn_w_gate_up[i].astype(BF16),
                           ffn_w_down[i].astype(BF16), norm_mix[0 if last else i + 1], not last)
            xs = res[0]
            h = None if last else res[1]
        outs.append(xs)
    return outs[0][None] if b == 1 else jnp.stack(outs, axis=0)
```

```python
import functools
import math
from typing import Any, NamedTuple

import numpy as np
import jax
import jax.numpy as jnp
from jax import lax
from jax.experimental import pallas as pl
from jax.experimental.pallas import tpu as pltpu

F32 = jnp.float32
BF16 = jnp.bfloat16

D_MODEL = 1024
NORM_EPS = 1e-6
ROPE_THETA = 10000.0
HGRN_HEADS, HGRN_DK, HGRN_DV = 8, 128, 128
GLA_HEADS, GLA_DK, GLA_DV = 4, 128, 256
GLA_KEY_DIM, GLA_VALUE_DIM, GLA_GATE_RANK = 512, 1024, 16
GLA_GATE_NORMALIZER = 16.0
RET_HEADS, RET_DK, RET_DV = 8, 128, 256
MLA_HEADS, MLA_Q_LORA, MLA_KV_LORA = 8, 384, 128
MLA_NOPE, MLA_ROPE, MLA_V = 128, 64, 128
MLA_QK = MLA_NOPE + MLA_ROPE
FFN_HIDDEN = 2816

LANE = 128
VMEM_LIMIT_BYTES = 56 * 1024 * 1024

SCAN_CHUNK = 128
PROJ_TM, PROJ_TN = 1024, 1024
FFN_TM = 512
MLA_TM = 256
HGRN_GROUP, GLA_GROUP, RET_GROUP = 2, 4, 8
ATTN_TK = 512
ATTN_SUB = 512

_NT = (((1,), (1,)), ((), ()))
_TN = (((0,), (0,)), ((), ()))


def _cparams(n_axes):
    return pltpu.CompilerParams(dimension_semantics=("arbitrary",) * n_axes,
                                vmem_limit_bytes=VMEM_LIMIT_BYTES)


def _rms(x, width=None):
    width = x.shape[-1] if width is None else width
    ss = jnp.sum(x * x, axis=-1, keepdims=True)
    return x * lax.rsqrt(ss * (1.0 / width) + NORM_EPS)


def _silu(x):
    return x * jax.nn.sigmoid(x)


def _rmsnorm_kernel(x_ref, g_ref, o_ref):
    o_ref[...] = (_rms(x_ref[...]) * g_ref[...]).astype(o_ref.dtype)


def _rmsnorm(x, gain, tm=1024):
    s, d = x.shape
    tm = min(tm, s)
    return pl.pallas_call(
        _rmsnorm_kernel,
        grid=(s // tm,),
        in_specs=[pl.BlockSpec((tm, d), lambda i: (i, 0)),
                  pl.BlockSpec((1, d), lambda i: (0, 0))],
        out_specs=pl.BlockSpec((tm, d), lambda i: (i, 0)),
        out_shape=jax.ShapeDtypeStruct((s, d), BF16),
        compiler_params=_cparams(1),
        name="rmsnorm",
    )(x, gain.reshape(1, d))


def _proj_kernel(h_ref, w_ref, o_ref):
    acc = jnp.dot(h_ref[...], w_ref[...].astype(BF16), preferred_element_type=F32)
    for c in range(acc.shape[1] // LANE):
        o_ref[c] = acc[:, c * LANE:(c + 1) * LANE].astype(o_ref.dtype)


def _proj(h, w, n_tiles, col_tile, out_dtype):
    s, k = h.shape
    tm = min(PROJ_TM, s)
    tn = min(PROJ_TN, w.shape[1])
    return pl.pallas_call(
        _proj_kernel,
        grid=(n_tiles, s // tm),
        in_specs=[pl.BlockSpec((tm, k), lambda j, i: (i, 0)),
                  pl.BlockSpec((k, tn), lambda j, i: (0, col_tile(j)))],
        out_specs=pl.BlockSpec((tn // LANE, tm, LANE), lambda j, i: (j, i, 0)),
        out_shape=jax.ShapeDtypeStruct((n_tiles * tn // LANE, s, LANE), out_dtype),
        compiler_params=_cparams(2),
        name="proj",
    )(h, w)


N_MATMUL_LEVELS = 3


def _scan_tables(chunk):
    n_levels = int(math.log2(chunk))
    t = np.arange(chunk)[:, None]
    r = np.arange(chunk)[None, :]
    blocks = [r <= t]
    level = np.full((chunk, chunk), -1, np.int32)
    level[np.arange(chunk), np.arange(chunk)] = n_levels
    for l in range(n_levels):
        m = 1 << l
        b = (t // (2 * m)) * (2 * m) + m - 1
        if 1 <= l < N_MATMUL_LEVELS:
            blocks.append(np.where(t > b, (r > b) & (r <= t), (r > t) & (r <= b)))
        same = (t // (2 * m)) == (r // (2 * m))
        level[same & (t % (2 * m) >= m) & (r % (2 * m) < m)] = l
    w = np.concatenate(blocks, axis=0).astype(np.float32)
    wcat = np.concatenate([w, w], axis=1)
    return jnp.asarray(wcat, BF16), jnp.asarray(level)


def _gate_sums(g, wcat_ref, cum_ref):
    c = g.shape[0]
    g_hi = g.astype(BF16)
    g_lo = (g - g_hi.astype(F32)).astype(BF16)
    gcat = jnp.concatenate([g_hi, g_lo], axis=0)
    sums = jnp.dot(wcat_ref[...], gcat, preferred_element_type=F32)
    cum_ref[...] = sums[:c]
    return sums[c:]


def _level_exponent(l, g, low, cum_ref, lanes):
    c = g.shape[0]
    if l == 0:
        odd = lax.broadcasted_iota(jnp.int32, g.shape, 0) % 2 == 1
        return jnp.where(odd, g, 0.0)
    if l < N_MATMUL_LEVELS:
        return low[(l - 1) * c:l * c]
    m = 1 << l
    parts = []
    for a in range(0, c, 2 * m):
        edge = cum_ref[a + m - 1:a + m, lanes]
        parts.append(edge - cum_ref[a:a + m, lanes])
        parts.append(cum_ref[a + m:a + 2 * m, lanes] - edge)
    return jnp.concatenate(parts, axis=0)


class _Head(NamedTuple):
    q: jax.Array
    k: jax.Array
    vb: jax.Array
    g: jax.Array
    low: jax.Array
    lanes: slice
    st_ref: Any


def _gated_chunk(heads, cum_ref, lv):
    c = heads[0].q.shape[0]
    n_levels = int(math.log2(c))
    scores = [jnp.where(lv == n_levels,
                        lax.dot_general(h.q.astype(BF16), h.k.astype(BF16), _NT,
                                        preferred_element_type=F32), 0.0) for h in heads]
    for l in range(n_levels):
        for n, h in enumerate(heads):
            e = jnp.exp(_level_exponent(l, h.g, h.low, cum_ref, h.lanes))
            s = jnp.dot((h.q * e).astype(BF16), (h.k * e).T.astype(BF16), preferred_element_type=F32)
            scores[n] = jnp.where(lv == l, s, scores[n])
    outs = []
    for h, sc in zip(heads, scores):
        cum = cum_ref[:, h.lanes]
        last = cum_ref[c - 1:c, h.lanes]
        q_in = (h.q * jnp.exp(cum)).astype(BF16)
        k_out = (h.k * jnp.exp(last - cum)).astype(BF16)
        st = h.st_ref[...]
        o = jnp.dot(sc.astype(BF16), h.vb, preferred_element_type=F32)
        o = o + lax.dot_general(q_in, st.astype(BF16), _NT, preferred_element_type=F32)
        h.st_ref[...] = st * jnp.exp(last) + lax.dot_general(h.vb, k_out, _TN,
                                                             preferred_element_type=F32)
        outs.append(o)
    return outs


def _reset_state(st_ref):
    @pl.when(pl.program_id(0) == 0)
    def _():
        st_ref[...] = jnp.zeros_like(st_ref)


def _cols(p_ref, first, n):
    if n == 1:
        return p_ref[first]
    return jnp.concatenate([p_ref[first + i] for i in range(n)], axis=1)


def _hgrn_kernel(p_ref, f_ref, lb_ref, gn_ref, wcat_ref, lv_ref, o_ref, st_ref, cum_ref):
    _reset_state(st_ref)
    nh = HGRN_HEADS
    lb = lb_ref[...]
    forget = lb + (1.0 - lb) * jax.nn.sigmoid(_cols(f_ref, 0, nh))
    g = jnp.log(forget)
    low = _gate_sums(g, wcat_ref, cum_ref)
    lv = lv_ref[...]
    for h0 in range(0, nh, HGRN_GROUP):
        group = range(h0, h0 + HGRN_GROUP)
        heads = []
        for hd in group:
            lanes = slice(hd * LANE, (hd + 1) * LANE)
            q = _silu(p_ref[hd].astype(F32)) * (HGRN_DK ** -0.5)
            heads.append(_Head(q, 1.0 - forget[:, lanes], p_ref[nh + hd], g[:, lanes], low[:, lanes],
                               lanes, st_ref.at[hd]))
        for hd, o in zip(group, _gated_chunk(heads, cum_ref, lv)):
            o = _rms(o) * gn_ref[...] * _silu(p_ref[2 * nh + hd].astype(F32))
            o_ref[:, hd * LANE:(hd + 1) * LANE] = o.astype(o_ref.dtype)


def _scan_call(body, p, pg, consts, n_heads, dk, dv, name):
    s = p.shape[1]
    c = min(SCAN_CHUNK, s)
    const = lambda a: pl.BlockSpec(a.shape, lambda t: (0,) * a.ndim)
    blocks = lambda a: pl.BlockSpec((a.shape[0], c, LANE), lambda t: (0, t, 0))
    return pl.pallas_call(
        body,
        grid=(s // c,),
        in_specs=[blocks(p), blocks(pg)] + [const(a) for a in consts],
        out_specs=pl.BlockSpec((c, n_heads * dv), lambda t: (t, 0)),
        out_shape=jax.ShapeDtypeStruct((s, n_heads * dv), BF16),
        scratch_shapes=[pltpu.VMEM((n_heads, dv, dk), F32),
                        pltpu.VMEM((c, n_heads * dk), F32)],
        compiler_params=_cparams(1),
        name=name,
    )(p, pg, *consts)


def _hgrn_scan(p, pf, lower_bound, g_norm):
    wcat, level = _scan_tables(min(SCAN_CHUNK, p.shape[1]))
    consts = (lower_bound.reshape(1, -1), g_norm.reshape(1, HGRN_DV), wcat, level)
    return _scan_call(_hgrn_kernel, p, pf, consts, HGRN_HEADS, HGRN_DK, HGRN_DV, "hgrn_scan")


def _log_sigmoid(z):
    return jnp.minimum(z, 0.0) - jnp.log(1.0 + jnp.exp(-jnp.abs(z)))


def _gla_kernel(p_ref, low_ref, wup_ref, b_ref, gn_ref, wcat_ref, lv_ref, o_ref, st_ref, cum_ref):
    _reset_state(st_ref)
    nh = GLA_HEADS
    nv = GLA_DV // LANE
    z = jnp.dot(low_ref[0].astype(BF16), wup_ref[...], preferred_element_type=F32) + b_ref[...]
    g = _log_sigmoid(z) * (1.0 / GLA_GATE_NORMALIZER)
    low = _gate_sums(g, wcat_ref, cum_ref)
    lv = lv_ref[...]
    for h0 in range(0, nh, GLA_GROUP):
        group = range(h0, h0 + GLA_GROUP)
        heads = []
        for hd in group:
            lanes = slice(hd * LANE, (hd + 1) * LANE)
            q = p_ref[hd].astype(F32) * (GLA_DK ** -0.5)
            heads.append(_Head(q, p_ref[nh + hd].astype(F32), _cols(p_ref, 2 * nh + nv * hd, nv),
                               g[:, lanes], low[:, lanes], lanes, st_ref.at[hd]))
        for hd, o in zip(group, _gated_chunk(heads, cum_ref, lv)):
            gate = _cols(p_ref, 2 * nh + nv * nh + nv * hd, nv).astype(F32)
            o = _rms(o) * gn_ref[...] * _silu(gate)
            o_ref[:, hd * GLA_DV:(hd + 1) * GLA_DV] = o.astype(o_ref.dtype)


def _gla_scan(p, p_low, w_up, b_gk, g_norm):
    wcat, level = _scan_tables(min(SCAN_CHUNK, p.shape[1]))
    consts = (w_up, b_gk.reshape(1, -1), g_norm.reshape(1, GLA_DV), wcat, level)
    return _scan_call(_gla_kernel, p, p_low, consts, GLA_HEADS, GLA_DK, GLA_DV, "gla_scan")


def _ret_kernel(p_ref, cos_ref, sin_ref, dm_ref, qd_ref, kd_ref, cd_ref, o_ref, st_ref):
    _reset_state(st_ref)
    nh = RET_HEADS
    nv = RET_DV // LANE
    cos = cos_ref[...]
    sin = sin_ref[...]
    half = RET_DK // 2
    for h0 in range(0, nh, RET_GROUP):
        group = range(h0, h0 + RET_GROUP)
        qs, ks, vbs, scores, outs = {}, {}, {}, {}, {}
        for hd in group:
            q = p_ref[hd].astype(F32)
            k = p_ref[nh + hd].astype(F32)
            qs[hd] = q * cos + pltpu.roll(q, half, 1) * sin
            ks[hd] = (k * cos + pltpu.roll(k, half, 1) * sin) * (RET_DK ** -0.5)
            vbs[hd] = _cols(p_ref, 2 * nh + nv * hd, nv)
        for hd in group:
            scores[hd] = lax.dot_general(qs[hd].astype(BF16), ks[hd].astype(BF16), _NT,
                                         preferred_element_type=F32) * dm_ref[hd]
        for hd in group:
            st = st_ref[hd]
            o = jnp.dot(scores[hd].astype(BF16), vbs[hd], preferred_element_type=F32)
            o = o + lax.dot_general((qs[hd] * qd_ref[hd]).astype(BF16), st.astype(BF16), _NT,
                                    preferred_element_type=F32)
            st_ref[hd] = st * cd_ref[hd] + lax.dot_general(vbs[hd], (ks[hd] * kd_ref[hd]).astype(BF16),
                                                           _TN, preferred_element_type=F32)
            outs[hd] = o
        for hd in group:
            o = _rms(outs[hd]) * _silu(_cols(p_ref, 2 * nh + nv * nh + nv * hd, nv).astype(F32))
            o_ref[:, hd * RET_DV:(hd + 1) * RET_DV] = o.astype(o_ref.dtype)


def _rope_angles(positions, dim):
    half = dim // 2
    inv_freq = 1.0 / (ROPE_THETA ** (jnp.arange(half, dtype=F32) / half))
    ang = positions.astype(F32)[:, None] * inv_freq
    return jnp.cos(ang), jnp.sin(ang)


def _rope_tables(cos, sin, pad_to):
    pad = jnp.zeros((cos.shape[0], pad_to // 2 - cos.shape[1]), F32)
    cos_t = jnp.concatenate([cos, pad, cos, pad], axis=1)
    sin_t = jnp.concatenate([-sin, pad, sin, pad], axis=1)
    return cos_t, sin_t


def _ret_scan(p, cos, sin):
    nb, s, _ = p.shape
    c = min(SCAN_CHUNK, s)
    h = RET_HEADS
    cos_t, sin_t = _rope_tables(cos, sin, LANE)
    log_gamma = jnp.log(1.0 - 2.0 ** (-5.0 - jnp.arange(h, dtype=F32)))
    idx = jnp.arange(c, dtype=F32)
    causal = jnp.tril(jnp.ones((c, c), dtype=bool))
    lg = log_gamma[:, None, None]
    dm = jnp.exp(jnp.where(causal, lg * (idx[:, None] - idx[None, :]), -jnp.inf))
    ones = jnp.ones((1, 1, LANE), F32)
    qd = jnp.exp(lg * (idx[None, :, None] + 1.0)) * ones
    kd = jnp.exp(lg * (c - 1.0 - idx[None, :, None])) * ones
    cd = jnp.exp(lg * float(c)) * ones
    tab = pl.BlockSpec((c, LANE), lambda t: (t, 0))
    const = lambda a: pl.BlockSpec(a.shape, lambda t: (0,) * a.ndim)
    return pl.pallas_call(
        _ret_kernel,
        grid=(s // c,),
        in_specs=[pl.BlockSpec((nb, c, LANE), lambda t: (0, t, 0)), tab, tab,
                  const(dm), const(qd), const(kd), const(cd)],
        out_specs=pl.BlockSpec((c, h * RET_DV), lambda t: (t, 0)),
        out_shape=jax.ShapeDtypeStruct((s, h * RET_DV), BF16),
        scratch_shapes=[pltpu.VMEM((h, RET_DV, RET_DK), F32)],
        compiler_params=_cparams(1),
        name="ret_scan",
    )(p, cos_t, sin_t, dm, qd, kd, cd)


MLA_QK_PAD = 2 * LANE


def _mla_prep_kernel(h_ref, win_ref, gq_ref, gkv_ref, wuq_ref, wukt_ref, wuv_ref, gqn_ref, gkn_ref,
                     cos_ref, sin_ref, cos_t_ref, sin_t_ref, q_ref, k_ref, v_ref):
    nh = MLA_HEADS
    c = jnp.dot(h_ref[...], win_ref[...], preferred_element_type=F32)
    c_q = _rms(c[:, :MLA_Q_LORA]) * gq_ref[...]
    c_kv = (_rms(c[:, MLA_Q_LORA:MLA_Q_LORA + MLA_KV_LORA]) * gkv_ref[...]).astype(BF16)
    qf = jnp.dot(c_q.astype(BF16), wuq_ref[...], preferred_element_type=F32)
    vf = jnp.dot(c_kv, wuv_ref[...], preferred_element_type=F32)
    knt = lax.dot_general(wukt_ref[...], c_kv, _NT, preferred_element_type=F32)
    krt = c[:, MLA_Q_LORA + MLA_KV_LORA:].T
    kr_ss = jnp.sum(krt * krt, axis=0, keepdims=True)
    cos = cos_ref[...]
    sin = sin_ref[...]
    cos_t = cos_t_ref[...]
    sin_t = sin_t_ref[...]
    gqn = gqn_ref[...]
    gkn = gkn_ref[...]
    half = LANE // 2
    scale = MLA_QK ** -0.5 * math.log2(math.e)
    for hd in range(nh):
        q_n = qf[:, hd * LANE:(hd + 1) * LANE]
        q_r = qf[:, (nh + hd) * LANE:(nh + hd + 1) * LANE]
        ss = jnp.sum(q_n * q_n + q_r * q_r, axis=-1, keepdims=True)
        r = lax.rsqrt(ss * (1.0 / MLA_QK) + NORM_EPS)
        q_n = q_n * r * gqn[:, :LANE]
        q_r = q_r * r * gqn[:, LANE:]
        q_r = q_r * cos + pltpu.roll(q_r, half, 1) * sin
        q_ref[hd] = (jnp.concatenate([q_n, q_r], axis=1) * scale).astype(q_ref.dtype)
        k_n = knt[hd * LANE:(hd + 1) * LANE]
        ss = jnp.sum(k_n * k_n, axis=0, keepdims=True) + kr_ss
        r = lax.rsqrt(ss * (1.0 / MLA_QK) + NORM_EPS)
        k_n = k_n * r * gkn[:LANE]
        k_r = krt * r * gkn[LANE:]
        k_r = k_r * cos_t + jnp.concatenate([k_r[half:], k_r[:half]], axis=0) * sin_t
        k_ref[hd] = jnp.concatenate([k_n, k_r], axis=0).astype(k_ref.dtype)
        v_h = vf[:, hd * LANE:(hd + 1) * LANE]
        v_ref[hd] = jnp.concatenate([v_h, jnp.ones_like(v_h)], axis=1).astype(v_ref.dtype)


def _pad_rope_cols(w):
    half = MLA_ROPE // 2
    z = jnp.zeros(w.shape[:-1] + (LANE // 2 - half,), w.dtype)
    return jnp.concatenate([w[..., :half], z, w[..., half:], z], axis=-1)


def _mla_prep(h, cos, sin, w_in, g_q_lora, g_kv_lora, w_uq, w_ukv, g_qnorm, g_knorm):
    s = h.shape[0]
    tm = min(MLA_TM, s)
    nh = MLA_HEADS
    lat = MLA_Q_LORA + MLA_KV_LORA
    w_in_p = jnp.concatenate([w_in[:, :lat], _pad_rope_cols(w_in[:, lat:])], axis=1).astype(BF16)
    wq = w_uq.reshape(MLA_Q_LORA, nh, MLA_QK)
    wq_nope = wq[:, :, :MLA_NOPE].reshape(MLA_Q_LORA, nh * LANE)
    wq_rope = _pad_rope_cols(wq[:, :, MLA_NOPE:]).reshape(MLA_Q_LORA, nh * LANE)
    w_uq_p = jnp.concatenate([wq_nope, wq_rope], axis=1).astype(BF16)
    wkv = w_ukv.reshape(MLA_KV_LORA, nh, MLA_NOPE + MLA_V)
    w_uk_t = wkv[:, :, :MLA_NOPE].reshape(MLA_KV_LORA, nh * LANE).T.astype(BF16)
    w_uv = wkv[:, :, MLA_NOPE:].reshape(MLA_KV_LORA, nh * LANE).astype(BF16)
    pad_gain = lambda g: jnp.concatenate([g[:MLA_NOPE], _pad_rope_cols(g[MLA_NOPE:])])
    cos_t, sin_t = _rope_tables(cos, sin, LANE)
    const = lambda a: pl.BlockSpec(a.shape, lambda i: (0,) * a.ndim)
    args = (w_in_p, g_q_lora.reshape(1, -1), g_kv_lora.reshape(1, -1), w_uq_p, w_uk_t, w_uv,
            pad_gain(g_qnorm).reshape(1, -1), pad_gain(g_knorm).reshape(-1, 1))
    tab = pl.BlockSpec((tm, LANE), lambda i: (i, 0))
    tab_t = pl.BlockSpec((LANE, tm), lambda i: (0, i))
    return pl.pallas_call(
        _mla_prep_kernel,
        grid=(s // tm,),
        in_specs=([pl.BlockSpec((tm, D_MODEL), lambda i: (i, 0))] + [const(a) for a in args]
                  + [tab, tab, tab_t, tab_t]),
        out_specs=[pl.BlockSpec((nh, tm, MLA_QK_PAD), lambda i: (0, i, 0)),
                   pl.BlockSpec((nh, MLA_QK_PAD, tm), lambda i: (0, 0, i)),
                   pl.BlockSpec((nh, tm, 2 * MLA_V), lambda i: (0, i, 0))],
        out_shape=[jax.ShapeDtypeStruct((nh, s, MLA_QK_PAD), BF16),
                   jax.ShapeDtypeStruct((nh, MLA_QK_PAD, s), BF16),
                   jax.ShapeDtypeStruct((nh, s, 2 * MLA_V), BF16)],
        compiler_params=_cparams(1),
        name="mla_prep",
    )(h, *args, cos_t, sin_t, cos_t.T, sin_t.T)


def _flash_kernel(q_ref, qn_ref, k_ref, v_ref, o_ref, s_ref, m_ref, acc_ref, *, sub, tk):
    r = sub // tk
    i = pl.program_id(1)
    m_ref[...] = jnp.full(m_ref.shape, -jnp.inf, F32)
    acc_ref[...] = jnp.zeros(acc_ref.shape, F32)

    def scores(u, j, dst, src_ref=q_ref):
        off = pl.multiple_of(j * tk, tk)
        s_ref[dst, u] = jnp.dot(src_ref[0, u * sub:(u + 1) * sub, :], k_ref[0, :, pl.ds(off, tk)],
                                preferred_element_type=F32)

    def accumulate(u, j, src, shift):
        off = pl.multiple_of(j * tk, tk)
        s = s_ref[src, u]
        if shift is not None:
            row = lax.broadcasted_iota(jnp.int32, (sub, tk), 0)
            col = lax.broadcasted_iota(jnp.int32, (sub, tk), 1)
            s = jnp.where(col - row <= shift, s, -jnp.inf)
        m_prev = m_ref[u]
        m_new = jnp.maximum(m_prev, jnp.max(s, axis=-1, keepdims=True))
        p = jnp.exp2(s - m_new).astype(BF16)
        pv = jnp.dot(p, v_ref[0, pl.ds(off, tk), :], preferred_element_type=F32)
        acc_ref[u] = jnp.exp2(m_prev - m_new) * acc_ref[u] + pv
        m_ref[u] = m_new

    @pl.when(i == 0)
    def _():
        for u in range(2):
            scores(u, 0, 0)

    def body(jj, carry):
        for u in range(2):
            scores(u, 2 * jj + 1, 1)
        for u in range(2):
            accumulate(u, 2 * jj, 0, None)
        for u in range(2):
            scores(u, 2 * jj + 2, 0)
        for u in range(2):
            accumulate(u, 2 * jj + 1, 1, None)
        return carry

    lax.fori_loop(0, r * i, body, 0)
    base = 2 * r * i
    for d in range(2 * r):
        if d + 1 < 2 * r:
            for u in range(2):
                if d + 1 < r * (u + 1):
                    scores(u, base + d + 1, (d + 1) % 2)
        else:
            for u in range(2):
                scores(u, 0, 0, qn_ref)
        for u in range(2):
            if d < r * (u + 1):
                accumulate(u, base + d, d % 2, None if d < r * u else (r * u - d) * tk)
    for u in range(2):
        acc = acc_ref[u]
        o_ref[u * sub:(u + 1) * sub, :] = (acc[:, :MLA_V] / acc[:, MLA_V:]).astype(o_ref.dtype)


def _flash(q, k, v):
    nh, s, _ = q.shape
    sub = min(ATTN_SUB, s // 2)
    tk = min(ATTN_TK, sub)
    tq = 2 * sub
    n_blocks = s // tq
    q_spec = lambda index: pl.BlockSpec((1, tq, MLA_QK_PAD), index)
    return pl.pallas_call(
        functools.partial(_flash_kernel, sub=sub, tk=tk),
        grid=(nh, n_blocks),
        in_specs=[q_spec(lambda h, i: (h, i, 0)),
                  q_spec(lambda h, i: (h, jnp.minimum(i + 1, n_blocks - 1), 0)),
                  pl.BlockSpec((1, MLA_QK_PAD, s), lambda h, i: (h, 0, 0)),
                  pl.BlockSpec((1, s, 2 * MLA_V), lambda h, i: (h, 0, 0))],
        out_specs=pl.BlockSpec((tq, MLA_V), lambda h, i: (i, h)),
        out_shape=jax.ShapeDtypeStruct((s, nh * MLA_V), BF16),
        scratch_shapes=[pltpu.VMEM((2, 2, sub, tk), F32),
                        pltpu.VMEM((2, sub, 1), F32),
                        pltpu.VMEM((2, sub, 2 * MLA_V), F32)],
        compiler_params=_cparams(2),
        name="flash",
    )(q, q, k, v)


def _out_ffn_kernel(a_ref, x_ref, wo_ref, gf_ref, wgu_ref, wd_ref, gn_ref, xo_ref, *maybe_ho_ref):
    x1 = x_ref[...] + jnp.dot(a_ref[...], wo_ref[...], preferred_element_type=F32)
    h = (_rms(x1) * gf_ref[...]).astype(BF16)
    au = jnp.dot(h, wgu_ref[...], preferred_element_type=F32)
    p = (_silu(au[:, :FFN_HIDDEN]) * au[:, FFN_HIDDEN:]).astype(BF16)
    x2 = x1 + jnp.dot(p, wd_ref[...], preferred_element_type=F32)
    xo_ref[...] = x2
    for ho_ref in maybe_ho_ref:
        ho_ref[...] = (_rms(x2) * gn_ref[...]).astype(ho_ref.dtype)


def _out_ffn(a, x, w_out, g_ffn, w_gate_up, w_down, layer, g_next, emit_next):
    s, d = x.shape
    tm = min(FFN_TM, s)
    row = lambda width: pl.BlockSpec((tm, width), lambda i: (i, 0))
    resident = lambda arr: pl.BlockSpec(arr.shape, lambda i: (0, 0), pipeline_mode=pl.Buffered(1))
    of_layer = lambda arr: pl.BlockSpec((None,) + arr.shape[1:], lambda i: (layer, 0, 0),
                                        pipeline_mode=pl.Buffered(1))
    gf = g_ffn.reshape(1, d)
    gn = g_next.reshape(1, d)
    n_out = 2 if emit_next else 1
    return pl.pallas_call(
        _out_ffn_kernel,
        grid=(s // tm,),
        in_specs=[row(a.shape[1]), row(d), resident(w_out), resident(gf), of_layer(w_gate_up),
                  of_layer(w_down), resident(gn)],
        out_specs=[row(d), row(d)][:n_out],
        out_shape=[jax.ShapeDtypeStruct((s, d), F32), jax.ShapeDtypeStruct((s, d), BF16)][:n_out],
        compiler_params=_cparams(1),
        name="out_ffn",
    )(a, x, w_out, gf, w_gate_up, w_down, gn)


def kernel(x, positions, norm_mix, norm_ffn, hgrn_w_in, hgrn_g_norm, hgrn_w_out, hgrn_lb_logits,
           gla_w_in, gla_w_gk_up, gla_b_gk, gla_g_norm, gla_w_out, ret_w_in, ret_w_out, mla_w_in,
           mla_g_q_lora, mla_g_kv_lora, mla_w_uq, mla_w_ukv, mla_g_qnorm, mla_g_knorm, mla_w_out,
           ffn_w_gate_up, ffn_w_down):
    b, s, d = x.shape
    depth = norm_mix.shape[0]
    lower_bounds = jnp.cumsum(jax.nn.softmax(hgrn_lb_logits.astype(F32), axis=0), axis=0)
    w_gate_up = ffn_w_gate_up.astype(BF16)
    w_down = ffn_w_down.astype(BF16)
    outs = []
    for bi in range(b):
        xs = x[bi]
        pos = positions[bi]
        cos, sin = _rope_angles(pos, RET_DK)
        step = RET_DK // MLA_ROPE
        h = _rmsnorm(xs, norm_mix[0])
        for i in range(depth):
            mixer, j = i % 4, i // 4
            if mixer == 0:
                w = hgrn_w_in[j]
                p = _proj(h, w, 3, lambda t: t + jnp.minimum(t, 1), BF16)
                pf = _proj(h, w, 1, lambda t: t + 1, F32)
                a = _hgrn_scan(p, pf, lower_bounds[i], hgrn_g_norm[j])
                w_out = hgrn_w_out[j]
            elif mixer == 1:
                main = 2 * GLA_KEY_DIM + 2 * GLA_VALUE_DIM
                pad = LANE - GLA_GATE_RANK
                p = _proj(h, gla_w_in[j], main // PROJ_TN, lambda t: t, BF16)
                w_low = jnp.pad(gla_w_in[j][:, main:], ((0, 0), (0, pad)))
                p_low = _proj(h, w_low, 1, lambda t: t, F32)
                w_up = jnp.pad(gla_w_gk_up[j], ((0, pad), (0, 0))).astype(BF16)
                a = _gla_scan(p, p_low, w_up, gla_b_gk[j], gla_g_norm[j])
                w_out = gla_w_out[j]
            elif mixer == 2:
                p = _proj(h, ret_w_in[j], ret_w_in.shape[2] // PROJ_TN, lambda t: t, BF16)
                a = _ret_scan(p, cos, sin)
                w_out = ret_w_out[j]
            else:
                q, k, v = _mla_prep(h, cos[:, ::step], sin[:, ::step], mla_w_in[j], mla_g_q_lora[j], mla_g_kv_lora[j],
                                    mla_w_uq[j], mla_w_ukv[j], mla_g_qnorm[j], mla_g_knorm[j])
                a = _flash(q, k, v)
                w_out = mla_w_out[j]
            last = i + 1 == depth
            res = _out_ffn(a, xs, w_out.astype(BF16), norm_ffn[i], w_gate_up, w_down, i,
                           norm_mix[0 if last else i + 1], not last)
            xs = res[0]
            h = None if last else res[1]
        outs.append(xs)
    return outs[0][None] if b == 1 else jnp.stack(outs, axis=0)
```

```python
import functools
import math
from typing import Any, NamedTuple

import numpy as np
import jax
import jax.numpy as jnp
from jax import lax
from jax.experimental import pallas as pl
from jax.experimental.pallas import tpu as pltpu

F32 = jnp.float32
BF16 = jnp.bfloat16

D_MODEL = 1024
NORM_EPS = 1e-6
ROPE_THETA = 10000.0
HGRN_HEADS, HGRN_DK, HGRN_DV = 8, 128, 128
GLA_HEADS, GLA_DK, GLA_DV = 4, 128, 256
GLA_KEY_DIM, GLA_VALUE_DIM, GLA_GATE_RANK = 512, 1024, 16
GLA_GATE_NORMALIZER = 16.0
RET_HEADS, RET_DK, RET_DV = 8, 128, 256
MLA_HEADS, MLA_Q_LORA, MLA_KV_LORA = 8, 384, 128
MLA_NOPE, MLA_ROPE, MLA_V = 128, 64, 128
MLA_QK = MLA_NOPE + MLA_ROPE
FFN_HIDDEN = 2816

LANE = 128
VMEM_LIMIT_BYTES = 56 * 1024 * 1024

SCAN_CHUNK = 128
SCAN_STEP_CHUNKS = 2
PROJ_TM, PROJ_TN = 2048, 1024
FFN_TM = 512
MLA_TM = 256
HGRN_GROUP, GLA_GROUP = 2, 4
ATTN_TK = 512
ATTN_SUB = 512

_NT = (((1,), (1,)), ((), ()))
_TN = (((0,), (0,)), ((), ()))


def _cparams(n_axes):
    return pltpu.CompilerParams(dimension_semantics=("arbitrary",) * n_axes,
                                vmem_limit_bytes=VMEM_LIMIT_BYTES)


def _rms(x, width=None):
    width = x.shape[-1] if width is None else width
    ss = jnp.sum(x * x, axis=-1, keepdims=True)
    return x * lax.rsqrt(ss * (1.0 / width) + NORM_EPS)


def _silu(x):
    return x * jax.nn.sigmoid(x)


def _rmsnorm_kernel(x_ref, g_ref, o_ref):
    o_ref[...] = (_rms(x_ref[...]) * g_ref[...]).astype(o_ref.dtype)


def _rmsnorm(x, gain, tm=1024):
    s, d = x.shape
    tm = min(tm, s)
    return pl.pallas_call(
        _rmsnorm_kernel,
        grid=(s // tm,),
        in_specs=[pl.BlockSpec((tm, d), lambda i: (i, 0)),
                  pl.BlockSpec((1, d), lambda i: (0, 0))],
        out_specs=pl.BlockSpec((tm, d), lambda i: (i, 0)),
        out_shape=jax.ShapeDtypeStruct((s, d), BF16),
        compiler_params=_cparams(1),
        name="rmsnorm",
    )(x, gain.reshape(1, d))


def _proj_kernel(h_ref, w_ref, o_ref):
    acc = jnp.dot(h_ref[...], w_ref[...].astype(BF16), preferred_element_type=F32)
    for c in range(acc.shape[1] // LANE):
        o_ref[c] = acc[:, c * LANE:(c + 1) * LANE].astype(o_ref.dtype)


def _proj(h, w, n_tiles, col_tile, out_dtype):
    s, k = h.shape
    tm = min(PROJ_TM, s)
    tn = min(PROJ_TN, w.shape[1])
    return pl.pallas_call(
        _proj_kernel,
        grid=(n_tiles, s // tm),
        in_specs=[pl.BlockSpec((tm, k), lambda j, i: (i, 0)),
                  pl.BlockSpec((k, tn), lambda j, i: (0, col_tile(j)))],
        out_specs=pl.BlockSpec((tn // LANE, tm, LANE), lambda j, i: (j, i, 0)),
        out_shape=jax.ShapeDtypeStruct((n_tiles * tn // LANE, s, LANE), out_dtype),
        compiler_params=_cparams(2),
        name="proj",
    )(h, w)


N_MATMUL_LEVELS = 3


def _scan_tables(chunk):
    n_levels = int(math.log2(chunk))
    t = np.arange(chunk)[:, None]
    r = np.arange(chunk)[None, :]
    blocks = [r <= t]
    level = np.full((chunk, chunk), -1, np.int32)
    level[np.arange(chunk), np.arange(chunk)] = n_levels
    for l in range(n_levels):
        m = 1 << l
        b = (t // (2 * m)) * (2 * m) + m - 1
        if 1 <= l < N_MATMUL_LEVELS:
            blocks.append(np.where(t > b, (r > b) & (r <= t), (r > t) & (r <= b)))
        same = (t // (2 * m)) == (r // (2 * m))
        level[same & (t % (2 * m) >= m) & (r % (2 * m) < m)] = l
    w = np.concatenate(blocks, axis=0).astype(np.float32)
    wcat = np.concatenate([w, w], axis=1)
    return jnp.asarray(wcat, BF16), jnp.asarray(level)


def _gate_sums(g, wcat_ref, cum_ref):
    c = g.shape[0]
    g_hi = g.astype(BF16)
    g_lo = (g - g_hi.astype(F32)).astype(BF16)
    gcat = jnp.concatenate([g_hi, g_lo], axis=0)
    sums = jnp.dot(wcat_ref[...], gcat, preferred_element_type=F32)
    cum_ref[...] = sums[:c]
    return sums[c:]


def _level_exponent(l, g, low, cum_ref, lanes):
    c = g.shape[0]
    if l == 0:
        odd = lax.broadcasted_iota(jnp.int32, g.shape, 0) % 2 == 1
        return jnp.where(odd, g, 0.0)
    if l < N_MATMUL_LEVELS:
        return low[(l - 1) * c:l * c]
    m = 1 << l
    parts = []
    for a in range(0, c, 2 * m):
        edge = cum_ref[a + m - 1:a + m, lanes]
        parts.append(edge - cum_ref[a:a + m, lanes])
        parts.append(cum_ref[a + m:a + 2 * m, lanes] - edge)
    return jnp.concatenate(parts, axis=0)


class _Head(NamedTuple):
    q: jax.Array
    k: jax.Array
    vb: jax.Array
    g: jax.Array
    low: jax.Array
    lanes: slice
    st_ref: Any


def _gated_intra(heads, cum_ref, lv):
    c = heads[0].q.shape[0]
    n_levels = int(math.log2(c))
    scores = [jnp.where(lv == n_levels,
                        lax.dot_general(h.q.astype(BF16), h.k.astype(BF16), _NT,
                                        preferred_element_type=F32), 0.0) for h in heads]
    for l in range(n_levels):
        for n, h in enumerate(heads):
            e = jnp.exp(_level_exponent(l, h.g, h.low, cum_ref, h.lanes))
            s = jnp.dot((h.q * e).astype(BF16), (h.k * e).T.astype(BF16), preferred_element_type=F32)
            scores[n] = jnp.where(lv == l, s, scores[n])
    return [jnp.dot(sc.astype(BF16), h.vb, preferred_element_type=F32) for h, sc in zip(heads, scores)]


def _gated_carry(h, o_intra, cum_ref):
    c = h.q.shape[0]
    cum = cum_ref[:, h.lanes]
    last = cum_ref[c - 1:c, h.lanes]
    q_in = (h.q * jnp.exp(cum)).astype(BF16)
    k_out = (h.k * jnp.exp(last - cum)).astype(BF16)
    st = h.st_ref[...]
    o = o_intra + lax.dot_general(q_in, st.astype(BF16), _NT, preferred_element_type=F32)
    h.st_ref[...] = st * jnp.exp(last) + lax.dot_general(h.vb, k_out, _TN, preferred_element_type=F32)
    return o


def _reset_state(st_ref):
    @pl.when(pl.program_id(0) == 0)
    def _():
        st_ref[...] = jnp.zeros_like(st_ref)


def _cols(p_ref, first, n, rows):
    if n == 1:
        return p_ref[first, rows, :]
    return jnp.concatenate([p_ref[first + i, rows, :] for i in range(n)], axis=1)


def _chunk_rows(ref):
    return [slice(r, r + SCAN_CHUNK) for r in range(0, ref.shape[1], SCAN_CHUNK)]


def _hgrn_kernel(p_ref, f_ref, lb_ref, gn_ref, wcat_ref, lv_ref, o_ref, st_ref, cum_ref):
    _reset_state(st_ref)
    nh = HGRN_HEADS
    lb = lb_ref[...]
    lv = lv_ref[...]
    work = []
    for ci, rows in enumerate(_chunk_rows(p_ref)):
        forget = lb + (1.0 - lb) * jax.nn.sigmoid(_cols(f_ref, 0, nh, rows))
        g = jnp.log(forget)
        low = _gate_sums(g, wcat_ref, cum_ref.at[ci])
        for h0 in range(0, nh, HGRN_GROUP):
            heads = []
            for hd in range(h0, h0 + HGRN_GROUP):
                lanes = slice(hd * LANE, (hd + 1) * LANE)
                q = _silu(p_ref[hd, rows, :].astype(F32)) * (HGRN_DK ** -0.5)
                heads.append(_Head(q, 1.0 - forget[:, lanes], p_ref[nh + hd, rows, :], g[:, lanes],
                                   low[:, lanes], lanes, st_ref.at[hd]))
            work += [(ci, rows, h0 + n, h, o) for n, (h, o) in
                     enumerate(zip(heads, _gated_intra(heads, cum_ref.at[ci], lv)))]
    for ci, rows, hd, h, o_intra in work:
        o = _gated_carry(h, o_intra, cum_ref.at[ci])
        o = _rms(o) * gn_ref[...] * _silu(p_ref[2 * nh + hd, rows, :].astype(F32))
        o_ref[rows, hd * LANE:(hd + 1) * LANE] = o.astype(o_ref.dtype)


def _scan_call(body, p, pg, consts, n_heads, dk, dv, name):
    s = p.shape[1]
    c = min(SCAN_CHUNK, s)
    n_chunks = min(SCAN_STEP_CHUNKS, s // c)
    tm = c * n_chunks
    const = lambda a: pl.BlockSpec(a.shape, lambda t: (0,) * a.ndim)
    blocks = lambda a: pl.BlockSpec((a.shape[0], tm, LANE), lambda t: (0, t, 0))
    return pl.pallas_call(
        body,
        grid=(s // tm,),
        in_specs=[blocks(p), blocks(pg)] + [const(a) for a in consts],
        out_specs=pl.BlockSpec((tm, n_heads * dv), lambda t: (t, 0)),
        out_shape=jax.ShapeDtypeStruct((s, n_heads * dv), BF16),
        scratch_shapes=[pltpu.VMEM((n_heads, dv, dk), F32),
                        pltpu.VMEM((n_chunks, c, n_heads * dk), F32)],
        compiler_params=_cparams(1),
        name=name,
    )(p, pg, *consts)


def _hgrn_scan(p, pf, lower_bound, g_norm):
    wcat, level = _scan_tables(min(SCAN_CHUNK, p.shape[1]))
    consts = (lower_bound.reshape(1, -1), g_norm.reshape(1, HGRN_DV), wcat, level)
    return _scan_call(_hgrn_kernel, p, pf, consts, HGRN_HEADS, HGRN_DK, HGRN_DV, "hgrn_scan")


def _log_sigmoid(z):
    return jnp.minimum(z, 0.0) - jnp.log(1.0 + jnp.exp(-jnp.abs(z)))


def _gla_kernel(p_ref, low_ref, wup_ref, b_ref, gn_ref, wcat_ref, lv_ref, o_ref, st_ref, cum_ref):
    _reset_state(st_ref)
    nh = GLA_HEADS
    nv = GLA_DV // LANE
    lv = lv_ref[...]
    work = []
    for ci, rows in enumerate(_chunk_rows(p_ref)):
        z = jnp.dot(low_ref[0, rows, :].astype(BF16), wup_ref[...], preferred_element_type=F32) + b_ref[...]
        g = _log_sigmoid(z) * (1.0 / GLA_GATE_NORMALIZER)
        low = _gate_sums(g, wcat_ref, cum_ref.at[ci])
        for h0 in range(0, nh, GLA_GROUP):
            heads = []
            for hd in range(h0, h0 + GLA_GROUP):
                lanes = slice(hd * LANE, (hd + 1) * LANE)
                q = p_ref[hd, rows, :].astype(F32) * (GLA_DK ** -0.5)
                heads.append(_Head(q, p_ref[nh + hd, rows, :].astype(F32), _cols(p_ref, 2 * nh + nv * hd, nv, rows),
                                   g[:, lanes], low[:, lanes], lanes, st_ref.at[hd]))
            work += [(ci, rows, h0 + n, h, o) for n, (h, o) in
                     enumerate(zip(heads, _gated_intra(heads, cum_ref.at[ci], lv)))]
    for ci, rows, hd, h, o_intra in work:
        o = _gated_carry(h, o_intra, cum_ref.at[ci])
        gate = _cols(p_ref, 2 * nh + nv * nh + nv * hd, nv, rows).astype(F32)
        o = _rms(o) * gn_ref[...] * _silu(gate)
        o_ref[rows, hd * GLA_DV:(hd + 1) * GLA_DV] = o.astype(o_ref.dtype)


def _gla_scan(p, p_low, w_up, b_gk, g_norm):
    wcat, level = _scan_tables(min(SCAN_CHUNK, p.shape[1]))
    consts = (w_up, b_gk.reshape(1, -1), g_norm.reshape(1, GLA_DV), wcat, level)
    return _scan_call(_gla_kernel, p, p_low, consts, GLA_HEADS, GLA_DK, GLA_DV, "gla_scan")


def _ret_kernel(p_ref, cos_ref, sin_ref, dm_ref, qd_ref, kd_ref, cd_ref, o_ref, st_ref):
    _reset_state(st_ref)
    nh = RET_HEADS
    nv = RET_DV // LANE
    half = RET_DK // 2
    work = []
    for rows in _chunk_rows(p_ref):
        cos = cos_ref[rows, :]
        sin = sin_ref[rows, :]
        qkv = []
        for hd in range(nh):
            q = p_ref[hd, rows, :].astype(F32)
            k = p_ref[nh + hd, rows, :].astype(F32)
            q = q * cos + pltpu.roll(q, half, 1) * sin
            k = (k * cos + pltpu.roll(k, half, 1) * sin) * (RET_DK ** -0.5)
            qkv.append((q, k, _cols(p_ref, 2 * nh + nv * hd, nv, rows)))
        for hd, (q, k, vb) in enumerate(qkv):
            scores = lax.dot_general(q.astype(BF16), k.astype(BF16), _NT,
                                     preferred_element_type=F32) * dm_ref[hd]
            work.append((rows, hd, q, k, vb, jnp.dot(scores.astype(BF16), vb, preferred_element_type=F32)))
    for n in range(0, len(work), nh):
        outs = []
        for rows, hd, q, k, vb, o in work[n:n + nh]:
            st = st_ref[hd]
            o = o + lax.dot_general((q * qd_ref[hd]).astype(BF16), st.astype(BF16), _NT,
                                    preferred_element_type=F32)
            st_ref[hd] = st * cd_ref[hd] + lax.dot_general(vb, (k * kd_ref[hd]).astype(BF16), _TN,
                                                           preferred_element_type=F32)
            outs.append(o)
        for (rows, hd, *_), o in zip(work[n:n + nh], outs):
            o = _rms(o) * _silu(_cols(p_ref, 2 * nh + nv * nh + nv * hd, nv, rows).astype(F32))
            o_ref[rows, hd * RET_DV:(hd + 1) * RET_DV] = o.astype(o_ref.dtype)


def _rope_angles(positions, dim):
    half = dim // 2
    inv_freq = 1.0 / (ROPE_THETA ** (jnp.arange(half, dtype=F32) / half))
    ang = positions.astype(F32)[:, None] * inv_freq
    return jnp.cos(ang), jnp.sin(ang)


def _rope_tables(cos, sin, pad_to):
    pad = jnp.zeros((cos.shape[0], pad_to // 2 - cos.shape[1]), F32)
    cos_t = jnp.concatenate([cos, pad, cos, pad], axis=1)
    sin_t = jnp.concatenate([-sin, pad, sin, pad], axis=1)
    return cos_t, sin_t


def _ret_scan(p, cos, sin):
    nb, s, _ = p.shape
    c = min(SCAN_CHUNK, s)
    h = RET_HEADS
    cos_t, sin_t = _rope_tables(cos, sin, LANE)
    log_gamma = jnp.log(1.0 - 2.0 ** (-5.0 - jnp.arange(h, dtype=F32)))
    idx = jnp.arange(c, dtype=F32)
    causal = jnp.tril(jnp.ones((c, c), dtype=bool))
    lg = log_gamma[:, None, None]
    dm = jnp.exp(jnp.where(causal, lg * (idx[:, None] - idx[None, :]), -jnp.inf))
    ones = jnp.ones((1, 1, LANE), F32)
    qd = jnp.exp(lg * (idx[None, :, None] + 1.0)) * ones
    kd = jnp.exp(lg * (c - 1.0 - idx[None, :, None])) * ones
    cd = jnp.exp(lg * float(c)) * ones
    tm = c * min(SCAN_STEP_CHUNKS, s // c)
    tab = pl.BlockSpec((tm, LANE), lambda t: (t, 0))
    const = lambda a: pl.BlockSpec(a.shape, lambda t: (0,) * a.ndim)
    return pl.pallas_call(
        _ret_kernel,
        grid=(s // tm,),
        in_specs=[pl.BlockSpec((nb, tm, LANE), lambda t: (0, t, 0)), tab, tab,
                  const(dm), const(qd), const(kd), const(cd)],
        out_specs=pl.BlockSpec((tm, h * RET_DV), lambda t: (t, 0)),
        out_shape=jax.ShapeDtypeStruct((s, h * RET_DV), BF16),
        scratch_shapes=[pltpu.VMEM((h, RET_DV, RET_DK), F32)],
        compiler_params=_cparams(1),
        name="ret_scan",
    )(p, cos_t, sin_t, dm, qd, kd, cd)


MLA_QK_PAD = 2 * LANE


def _mla_prep_kernel(h_ref, win_ref, gq_ref, gkv_ref, wuq_ref, wukt_ref, wuv_ref, gqn_ref, gkn_ref,
                     cos_ref, sin_ref, cos_t_ref, sin_t_ref, q_ref, k_ref, v_ref):
    nh = MLA_HEADS
    c = jnp.dot(h_ref[...], win_ref[...], preferred_element_type=F32)
    c_q = _rms(c[:, :MLA_Q_LORA]) * gq_ref[...]
    c_kv = (_rms(c[:, MLA_Q_LORA:MLA_Q_LORA + MLA_KV_LORA]) * gkv_ref[...]).astype(BF16)
    qf = jnp.dot(c_q.astype(BF16), wuq_ref[...], preferred_element_type=F32)
    vf = jnp.dot(c_kv, wuv_ref[...], preferred_element_type=F32)
    knt = lax.dot_general(wukt_ref[...], c_kv, _NT, preferred_element_type=F32)
    krt = c[:, MLA_Q_LORA + MLA_KV_LORA:].T
    kr_ss = jnp.sum(krt * krt, axis=0, keepdims=True)
    cos = cos_ref[...]
    sin = sin_ref[...]
    cos_t = cos_t_ref[...]
    sin_t = sin_t_ref[...]
    gqn = gqn_ref[...]
    gkn = gkn_ref[...]
    half = LANE // 2
    scale = MLA_QK ** -0.5 * math.log2(math.e)
    for hd in range(nh):
        q_n = qf[:, hd * LANE:(hd + 1) * LANE]
        q_r = qf[:, (nh + hd) * LANE:(nh + hd + 1) * LANE]
        ss = jnp.sum(q_n * q_n + q_r * q_r, axis=-1, keepdims=True)
        r = lax.rsqrt(ss * (1.0 / MLA_QK) + NORM_EPS)
        q_n = q_n * r * gqn[:, :LANE]
        q_r = q_r * r * gqn[:, LANE:]
        q_r = q_r * cos + pltpu.roll(q_r, half, 1) * sin
        q_ref[hd] = (jnp.concatenate([q_n, q_r], axis=1) * scale).astype(q_ref.dtype)
        k_n = knt[hd * LANE:(hd + 1) * LANE]
        ss = jnp.sum(k_n * k_n, axis=0, keepdims=True) + kr_ss
        r = lax.rsqrt(ss * (1.0 / MLA_QK) + NORM_EPS)
        k_n = k_n * r * gkn[:LANE]
        k_r = krt * r * gkn[LANE:]
        k_r = k_r * cos_t + jnp.concatenate([k_r[half:], k_r[:half]], axis=0) * sin_t
        k_ref[hd] = jnp.concatenate([k_n, k_r], axis=0).astype(k_ref.dtype)
        v_h = vf[:, hd * LANE:(hd + 1) * LANE]
        v_ref[hd] = jnp.concatenate([v_h, jnp.ones_like(v_h)], axis=1).astype(v_ref.dtype)


def _pad_rope_cols(w):
    half = MLA_ROPE // 2
    z = jnp.zeros(w.shape[:-1] + (LANE // 2 - half,), w.dtype)
    return jnp.concatenate([w[..., :half], z, w[..., half:], z], axis=-1)


def _mla_prep(h, cos, sin, w_in, g_q_lora, g_kv_lora, w_uq, w_ukv, g_qnorm, g_knorm):
    s = h.shape[0]
    tm = min(MLA_TM, s)
    nh = MLA_HEADS
    lat = MLA_Q_LORA + MLA_KV_LORA
    w_in_p = jnp.concatenate([w_in[:, :lat], _pad_rope_cols(w_in[:, lat:])], axis=1).astype(BF16)
    wq = w_uq.reshape(MLA_Q_LORA, nh, MLA_QK)
    wq_nope = wq[:, :, :MLA_NOPE].reshape(MLA_Q_LORA, nh * LANE)
    wq_rope = _pad_rope_cols(wq[:, :, MLA_NOPE:]).reshape(MLA_Q_LORA, nh * LANE)
    w_uq_p = jnp.concatenate([wq_nope, wq_rope], axis=1).astype(BF16)
    wkv = w_ukv.reshape(MLA_KV_LORA, nh, MLA_NOPE + MLA_V)
    w_uk_t = wkv[:, :, :MLA_NOPE].reshape(MLA_KV_LORA, nh * LANE).T.astype(BF16)
    w_uv = wkv[:, :, MLA_NOPE:].reshape(MLA_KV_LORA, nh * LANE).astype(BF16)
    pad_gain = lambda g: jnp.concatenate([g[:MLA_NOPE], _pad_rope_cols(g[MLA_NOPE:])])
    cos_t, sin_t = _rope_tables(cos, sin, LANE)
    const = lambda a: pl.BlockSpec(a.shape, lambda i: (0,) * a.ndim)
    args = (w_in_p, g_q_lora.reshape(1, -1), g_kv_lora.reshape(1, -1), w_uq_p, w_uk_t, w_uv,
            pad_gain(g_qnorm).reshape(1, -1), pad_gain(g_knorm).reshape(-1, 1))
    tab = pl.BlockSpec((tm, LANE), lambda i: (i, 0))
    tab_t = pl.BlockSpec((LANE, tm), lambda i: (0, i))
    return pl.pallas_call(
        _mla_prep_kernel,
        grid=(s // tm,),
        in_specs=([pl.BlockSpec((tm, D_MODEL), lambda i: (i, 0))] + [const(a) for a in args]
                  + [tab, tab, tab_t, tab_t]),
        out_specs=[pl.BlockSpec((nh, tm, MLA_QK_PAD), lambda i: (0, i, 0)),
                   pl.BlockSpec((nh, MLA_QK_PAD, tm), lambda i: (0, 0, i)),
                   pl.BlockSpec((nh, tm, 2 * MLA_V), lambda i: (0, i, 0))],
        out_shape=[jax.ShapeDtypeStruct((nh, s, MLA_QK_PAD), BF16),
                   jax.ShapeDtypeStruct((nh, MLA_QK_PAD, s), BF16),
                   jax.ShapeDtypeStruct((nh, s, 2 * MLA_V), BF16)],
        compiler_params=_cparams(1),
        name="mla_prep",
    )(h, *args, cos_t, sin_t, cos_t.T, sin_t.T)


def _flash_kernel(q_ref, qn_ref, k_ref, v_ref, o_ref, s_ref, m_ref, acc_ref, *, sub, tk):
    r = sub // tk
    i = pl.program_id(1)
    m_ref[...] = jnp.full(m_ref.shape, -jnp.inf, F32)
    acc_ref[...] = jnp.zeros(acc_ref.shape, F32)

    def scores(u, j, dst, src_ref=q_ref):
        off = pl.multiple_of(j * tk, tk)
        s_ref[dst, u] = jnp.dot(src_ref[0, u * sub:(u + 1) * sub, :], k_ref[0, :, pl.ds(off, tk)],
                                preferred_element_type=F32)

    def accumulate(u, j, src, shift):
        off = pl.multiple_of(j * tk, tk)
        s = s_ref[src, u]
        if shift is not None:
            row = lax.broadcasted_iota(jnp.int32, (sub, tk), 0)
            col = lax.broadcasted_iota(jnp.int32, (sub, tk), 1)
            s = jnp.where(col - row <= shift, s, -jnp.inf)
        m_prev = m_ref[u]
        m_new = jnp.maximum(m_prev, jnp.max(s, axis=-1, keepdims=True))
        p = jnp.exp2(s - m_new).astype(BF16)
        pv = jnp.dot(p, v_ref[0, pl.ds(off, tk), :], preferred_element_type=F32)
        acc_ref[u] = jnp.exp2(m_prev - m_new) * acc_ref[u] + pv
        m_ref[u] = m_new

    @pl.when(i == 0)
    def _():
        for u in range(2):
            scores(u, 0, 0)

    def body(jj, carry):
        for u in range(2):
            scores(u, 2 * jj + 1, 1)
        for u in range(2):
            accumulate(u, 2 * jj, 0, None)
        for u in range(2):
            scores(u, 2 * jj + 2, 0)
        for u in range(2):
            accumulate(u, 2 * jj + 1, 1, None)
        return carry

    lax.fori_loop(0, r * i, body, 0)
    base = 2 * r * i
    for d in range(2 * r):
        if d + 1 < 2 * r:
            for u in range(2):
                if d + 1 < r * (u + 1):
                    scores(u, base + d + 1, (d + 1) % 2)
        else:
            for u in range(2):
                scores(u, 0, 0, qn_ref)
        for u in range(2):
            if d < r * (u + 1):
                accumulate(u, base + d, d % 2, None if d < r * u else (r * u - d) * tk)
    for u in range(2):
        acc = acc_ref[u]
        o_ref[u * sub:(u + 1) * sub, :] = (acc[:, :MLA_V] / acc[:, MLA_V:]).astype(o_ref.dtype)


def _flash(q, k, v):
    nh, s, _ = q.shape
    sub = min(ATTN_SUB, s // 2)
    tk = min(ATTN_TK, sub)
    tq = 2 * sub
    n_blocks = s // tq
    q_spec = lambda index: pl.BlockSpec((1, tq, MLA_QK_PAD), index)
    return pl.pallas_call(
        functools.partial(_flash_kernel, sub=sub, tk=tk),
        grid=(nh, n_blocks),
        in_specs=[q_spec(lambda h, i: (h, i, 0)),
                  q_spec(lambda h, i: (h, jnp.minimum(i + 1, n_blocks - 1), 0)),
                  pl.BlockSpec((1, MLA_QK_PAD, s), lambda h, i: (h, 0, 0)),
                  pl.BlockSpec((1, s, 2 * MLA_V), lambda h, i: (h, 0, 0))],
        out_specs=pl.BlockSpec((tq, MLA_V), lambda h, i: (i, h)),
        out_shape=jax.ShapeDtypeStruct((s, nh * MLA_V), BF16),
        scratch_shapes=[pltpu.VMEM((2, 2, sub, tk), F32),
                        pltpu.VMEM((2, sub, 1), F32),
                        pltpu.VMEM((2, sub, 2 * MLA_V), F32)],
        compiler_params=_cparams(2),
        name="flash",
    )(q, q, k, v)


def _out_ffn_kernel(a_ref, x_ref, wo_ref, gf_ref, wgu_ref, wd_ref, gn_ref, xo_ref, *maybe_ho_ref):
    x1 = x_ref[...] + jnp.dot(a_ref[...], wo_ref[...], preferred_element_type=F32)
    h = (_rms(x1) * gf_ref[...]).astype(BF16)
    au = jnp.dot(h, wgu_ref[...], preferred_element_type=F32)
    p = (_silu(au[:, :FFN_HIDDEN]) * au[:, FFN_HIDDEN:]).astype(BF16)
    x2 = x1 + jnp.dot(p, wd_ref[...], preferred_element_type=F32)
    xo_ref[...] = x2
    for ho_ref in maybe_ho_ref:
        ho_ref[...] = (_rms(x2) * gn_ref[...]).astype(ho_ref.dtype)


def _out_ffn(a, x, w_out, g_ffn, w_gate_up, w_down, layer, g_next, emit_next):
    s, d = x.shape
    tm = min(FFN_TM, s)
    row = lambda width: pl.BlockSpec((tm, width), lambda i: (i, 0))
    resident = lambda arr: pl.BlockSpec(arr.shape, lambda i: (0, 0), pipeline_mode=pl.Buffered(1))
    of_layer = lambda arr: pl.BlockSpec((None,) + arr.shape[1:], lambda i: (layer, 0, 0),
                                        pipeline_mode=pl.Buffered(1))
    gf = g_ffn.reshape(1, d)
    gn = g_next.reshape(1, d)
    n_out = 2 if emit_next else 1
    return pl.pallas_call(
        _out_ffn_kernel,
        grid=(s // tm,),
        in_specs=[row(a.shape[1]), row(d), resident(w_out), resident(gf), of_layer(w_gate_up),
                  of_layer(w_down), resident(gn)],
        out_specs=[row(d), row(d)][:n_out],
        out_shape=[jax.ShapeDtypeStruct((s, d), F32), jax.ShapeDtypeStruct((s, d), BF16)][:n_out],
        compiler_params=_cparams(1),
        name="out_ffn",
    )(a, x, w_out, gf, w_gate_up, w_down, gn)


def kernel(x, positions, norm_mix, norm_ffn, hgrn_w_in, hgrn_g_norm, hgrn_w_out, hgrn_lb_logits,
           gla_w_in, gla_w_gk_up, gla_b_gk, gla_g_norm, gla_w_out, ret_w_in, ret_w_out, mla_w_in,
           mla_g_q_lora, mla_g_kv_lora, mla_w_uq, mla_w_ukv, mla_g_qnorm, mla_g_knorm, mla_w_out,
           ffn_w_gate_up, ffn_w_down):
    b, s, d = x.shape
    depth = norm_mix.shape[0]
    lower_bounds = jnp.cumsum(jax.nn.softmax(hgrn_lb_logits.astype(F32), axis=0), axis=0)
    w_gate_up = ffn_w_gate_up.astype(BF16)
    w_down = ffn_w_down.astype(BF16)
    outs = []
    for bi in range(b):
        xs = x[bi]
        pos = positions[bi]
        cos, sin = _rope_angles(pos, RET_DK)
        step = RET_DK // MLA_ROPE
        h = _rmsnorm(xs, norm_mix[0])
        for i in range(depth):
            mixer, j = i % 4, i // 4
            if mixer == 0:
                w = hgrn_w_in[j]
                p = _proj(h, w, 3, lambda t: t + jnp.minimum(t, 1), BF16)
                pf = _proj(h, w, 1, lambda t: t + 1, F32)
                a = _hgrn_scan(p, pf, lower_bounds[i], hgrn_g_norm[j])
                w_out = hgrn_w_out[j]
            elif mixer == 1:
                main = 2 * GLA_KEY_DIM + 2 * GLA_VALUE_DIM
                pad = LANE - GLA_GATE_RANK
                p = _proj(h, gla_w_in[j], main // PROJ_TN, lambda t: t, BF16)
                w_low = jnp.pad(gla_w_in[j][:, main:], ((0, 0), (0, pad)))
                p_low = _proj(h, w_low, 1, lambda t: t, F32)
                w_up = jnp.pad(gla_w_gk_up[j], ((0, pad), (0, 0))).astype(BF16)
                a = _gla_scan(p, p_low, w_up, gla_b_gk[j], gla_g_norm[j])
                w_out = gla_w_out[j]
            elif mixer == 2:
                p = _proj(h, ret_w_in[j], ret_w_in.shape[2] // PROJ_TN, lambda t: t, BF16)
                a = _ret_scan(p, cos, sin)
                w_out = ret_w_out[j]
            else:
                q, k, v = _mla_prep(h, cos[:, ::step], sin[:, ::step], mla_w_in[j], mla_g_q_lora[j], mla_g_kv_lora[j],
                                    mla_w_uq[j], mla_w_ukv[j], mla_g_qnorm[j], mla_g_knorm[j])
                a = _flash(q, k, v)
                w_out = mla_w_out[j]
            last = i + 1 == depth
            res = _out_ffn(a, xs, w_out.astype(BF16), norm_ffn[i], w_gate_up, w_down, i,
                           norm_mix[0 if last else i + 1], not last)
            xs = res[0]
            h = None if last else res[1]
        outs.append(xs)
    return outs[0][None] if b == 1 else jnp.stack(outs, axis=0)
```

```python
import functools
import math
from typing import Any, NamedTuple

import numpy as np
import jax
import jax.numpy as jnp
from jax import lax
from jax.experimental import pallas as pl
from jax.experimental.pallas import tpu as pltpu

F32 = jnp.float32
BF16 = jnp.bfloat16

D_MODEL = 1024
NORM_EPS = 1e-6
ROPE_THETA = 10000.0
HGRN_HEADS, HGRN_DK, HGRN_DV = 8, 128, 128
GLA_HEADS, GLA_DK, GLA_DV = 4, 128, 256
GLA_KEY_DIM, GLA_VALUE_DIM, GLA_GATE_RANK = 512, 1024, 16
GLA_GATE_NORMALIZER = 16.0
RET_HEADS, RET_DK, RET_DV = 8, 128, 256
MLA_HEADS, MLA_Q_LORA, MLA_KV_LORA = 8, 384, 128
MLA_NOPE, MLA_ROPE, MLA_V = 128, 64, 128
MLA_QK = MLA_NOPE + MLA_ROPE
FFN_HIDDEN = 2816

LANE = 128
VMEM_LIMIT_BYTES = 56 * 1024 * 1024

SCAN_CHUNK = 128
SCAN_STEP_CHUNKS = 4
PROJ_TM, PROJ_TN = 2048, 1024
FFN_TM = 512
MLA_TM = 256
ATTN_TK = 512
ATTN_SUB = 512

_NT = (((1,), (1,)), ((), ()))
_TN = (((0,), (0,)), ((), ()))


def _cparams(n_axes):
    return pltpu.CompilerParams(dimension_semantics=("arbitrary",) * n_axes,
                                vmem_limit_bytes=VMEM_LIMIT_BYTES)


def _rms(x, width=None):
    width = x.shape[-1] if width is None else width
    ss = jnp.sum(x * x, axis=-1, keepdims=True)
    return x * lax.rsqrt(ss * (1.0 / width) + NORM_EPS)


def _silu(x):
    return x * jax.nn.sigmoid(x)


def _rmsnorm_kernel(x_ref, g_ref, o_ref):
    o_ref[...] = (_rms(x_ref[...]) * g_ref[...]).astype(o_ref.dtype)


def _rmsnorm(x, gain, tm=1024):
    s, d = x.shape
    tm = min(tm, s)
    return pl.pallas_call(
        _rmsnorm_kernel,
        grid=(s // tm,),
        in_specs=[pl.BlockSpec((tm, d), lambda i: (i, 0)),
                  pl.BlockSpec((1, d), lambda i: (0, 0))],
        out_specs=pl.BlockSpec((tm, d), lambda i: (i, 0)),
        out_shape=jax.ShapeDtypeStruct((s, d), BF16),
        compiler_params=_cparams(1),
        name="rmsnorm",
    )(x, gain.reshape(1, d))


def _proj_kernel(h_ref, w_ref, o_ref):
    acc = jnp.dot(h_ref[...], w_ref[...].astype(BF16), preferred_element_type=F32)
    for c in range(acc.shape[1] // LANE):
        o_ref[c] = acc[:, c * LANE:(c + 1) * LANE].astype(o_ref.dtype)


def _proj(h, w, n_tiles, col_tile, out_dtype):
    s, k = h.shape
    tm = min(PROJ_TM, s)
    tn = min(PROJ_TN, w.shape[1])
    return pl.pallas_call(
        _proj_kernel,
        grid=(n_tiles, s // tm),
        in_specs=[pl.BlockSpec((tm, k), lambda j, i: (i, 0)),
                  pl.BlockSpec((k, tn), lambda j, i: (0, col_tile(j)))],
        out_specs=pl.BlockSpec((tn // LANE, tm, LANE), lambda j, i: (j, i, 0)),
        out_shape=jax.ShapeDtypeStruct((n_tiles * tn // LANE, s, LANE), out_dtype),
        compiler_params=_cparams(2),
        name="proj",
    )(h, w)


N_MATMUL_LEVELS = 3


def _scan_tables(chunk):
    n_levels = int(math.log2(chunk))
    t = np.arange(chunk)[:, None]
    r = np.arange(chunk)[None, :]
    blocks = [r <= t]
    level = np.full((chunk, chunk), -1, np.int32)
    level[np.arange(chunk), np.arange(chunk)] = n_levels
    for l in range(n_levels):
        m = 1 << l
        b = (t // (2 * m)) * (2 * m) + m - 1
        if 1 <= l < N_MATMUL_LEVELS:
            blocks.append(np.where(t > b, (r > b) & (r <= t), (r > t) & (r <= b)))
        same = (t // (2 * m)) == (r // (2 * m))
        level[same & (t % (2 * m) >= m) & (r % (2 * m) < m)] = l
    w = np.concatenate(blocks, axis=0).astype(np.float32)
    wcat = np.concatenate([w, w], axis=1)
    return jnp.asarray(wcat, BF16), jnp.asarray(level)


def _gate_sums(g, wcat_ref, cum_ref):
    c = g.shape[0]
    g_hi = g.astype(BF16)
    g_lo = (g - g_hi.astype(F32)).astype(BF16)
    gcat = jnp.concatenate([g_hi, g_lo], axis=0)
    sums = jnp.dot(wcat_ref[...], gcat, preferred_element_type=F32)
    cum_ref[...] = sums[:c]
    return sums[c:]


def _level_exponent(l, g, low, cum_ref, lanes):
    c = g.shape[0]
    if l == 0:
        odd = lax.broadcasted_iota(jnp.int32, g.shape, 0) % 2 == 1
        return jnp.where(odd, g, 0.0)
    if l < N_MATMUL_LEVELS:
        return low[(l - 1) * c:l * c]
    m = 1 << l
    parts = []
    for a in range(0, c, 2 * m):
        edge = cum_ref[a + m - 1:a + m, lanes]
        parts.append(edge - cum_ref[a:a + m, lanes])
        parts.append(cum_ref[a + m:a + 2 * m, lanes] - edge)
    return jnp.concatenate(parts, axis=0)


class _Head(NamedTuple):
    q: jax.Array
    k: jax.Array
    vb: jax.Array
    g: jax.Array
    low: jax.Array
    lanes: slice
    st_ref: Any


def _gated_intra(h, cum_ref, lv):
    c = h.q.shape[0]
    n_levels = int(math.log2(c))
    scores = jnp.where(lv == n_levels,
                       lax.dot_general(h.q.astype(BF16), h.k.astype(BF16), _NT,
                                       preferred_element_type=F32), 0.0)
    for l in range(n_levels):
        e = jnp.exp(_level_exponent(l, h.g, h.low, cum_ref, h.lanes))
        s = jnp.dot((h.q * e).astype(BF16), (h.k * e).T.astype(BF16), preferred_element_type=F32)
        scores = jnp.where(lv == l, s, scores)
    return jnp.dot(scores.astype(BF16), h.vb, preferred_element_type=F32)


def _gated_carry(h, o_intra, cum_ref):
    c = h.q.shape[0]
    cum = cum_ref[:, h.lanes]
    last = cum_ref[c - 1:c, h.lanes]
    q_in = (h.q * jnp.exp(cum)).astype(BF16)
    k_out = (h.k * jnp.exp(last - cum)).astype(BF16)
    st = h.st_ref[...]
    o = o_intra + lax.dot_general(q_in, st.astype(BF16), _NT, preferred_element_type=F32)
    h.st_ref[...] = st * jnp.exp(last) + lax.dot_general(h.vb, k_out, _TN, preferred_element_type=F32)
    return o


def _reset_state(st_ref):
    @pl.when(pl.program_id(0) == 0)
    def _():
        st_ref[...] = jnp.zeros_like(st_ref)


def _cols(p_ref, first, n, rows):
    if n == 1:
        return p_ref[first, rows, :]
    return jnp.concatenate([p_ref[first + i, rows, :] for i in range(n)], axis=1)


def _chunk_rows(ref):
    return [slice(r, r + SCAN_CHUNK) for r in range(0, ref.shape[1], SCAN_CHUNK)]


def _hgrn_kernel(p_ref, f_ref, lb_ref, gn_ref, wcat_ref, lv_ref, o_ref, st_ref, cum_ref):
    _reset_state(st_ref)
    nh = HGRN_HEADS
    lb = lb_ref[...]
    lv = lv_ref[...]
    work = []
    for ci, rows in enumerate(_chunk_rows(p_ref)):
        forget = lb + (1.0 - lb) * jax.nn.sigmoid(_cols(f_ref, 0, nh, rows))
        g = jnp.log(forget)
        low = _gate_sums(g, wcat_ref, cum_ref.at[ci])
        for hd in range(nh):
            lanes = slice(hd * LANE, (hd + 1) * LANE)
            q = _silu(p_ref[hd, rows, :].astype(F32)) * (HGRN_DK ** -0.5)
            h = _Head(q, 1.0 - forget[:, lanes], p_ref[nh + hd, rows, :], g[:, lanes], low[:, lanes],
                      lanes, st_ref.at[hd])
            work.append((ci, rows, hd, h, _gated_intra(h, cum_ref.at[ci], lv)))
    for ci, rows, hd, h, o_intra in work:
        o = _gated_carry(h, o_intra, cum_ref.at[ci])
        o = _rms(o) * gn_ref[...] * _silu(p_ref[2 * nh + hd, rows, :].astype(F32))
        o_ref[rows, hd * LANE:(hd + 1) * LANE] = o.astype(o_ref.dtype)


def _scan_call(body, p, pg, consts, n_heads, dk, dv, name):
    s = p.shape[1]
    c = min(SCAN_CHUNK, s)
    n_chunks = min(SCAN_STEP_CHUNKS, s // c)
    tm = c * n_chunks
    const = lambda a: pl.BlockSpec(a.shape, lambda t: (0,) * a.ndim)
    blocks = lambda a: pl.BlockSpec((a.shape[0], tm, LANE), lambda t: (0, t, 0))
    return pl.pallas_call(
        body,
        grid=(s // tm,),
        in_specs=[blocks(p), blocks(pg)] + [const(a) for a in consts],
        out_specs=pl.BlockSpec((tm, n_heads * dv), lambda t: (t, 0)),
        out_shape=jax.ShapeDtypeStruct((s, n_heads * dv), BF16),
        scratch_shapes=[pltpu.VMEM((n_heads, dv, dk), F32),
                        pltpu.VMEM((n_chunks, c, n_heads * dk), F32)],
        compiler_params=_cparams(1),
        name=name,
    )(p, pg, *consts)


def _hgrn_scan(p, pf, lower_bound, g_norm):
    wcat, level = _scan_tables(min(SCAN_CHUNK, p.shape[1]))
    consts = (lower_bound.reshape(1, -1), g_norm.reshape(1, HGRN_DV), wcat, level)
    return _scan_call(_hgrn_kernel, p, pf, consts, HGRN_HEADS, HGRN_DK, HGRN_DV, "hgrn_scan")


def _log_sigmoid(z):
    return jnp.minimum(z, 0.0) - jnp.log(1.0 + jnp.exp(-jnp.abs(z)))


def _gla_kernel(p_ref, low_ref, wup_ref, b_ref, gn_ref, wcat_ref, lv_ref, o_ref, st_ref, cum_ref):
    _reset_state(st_ref)
    nh = GLA_HEADS
    nv = GLA_DV // LANE
    lv = lv_ref[...]
    work = []
    for ci, rows in enumerate(_chunk_rows(p_ref)):
        z = jnp.dot(low_ref[0, rows, :].astype(BF16), wup_ref[...], preferred_element_type=F32) + b_ref[...]
        g = _log_sigmoid(z) * (1.0 / GLA_GATE_NORMALIZER)
        low = _gate_sums(g, wcat_ref, cum_ref.at[ci])
        for hd in range(nh):
            lanes = slice(hd * LANE, (hd + 1) * LANE)
            q = p_ref[hd, rows, :].astype(F32) * (GLA_DK ** -0.5)
            h = _Head(q, p_ref[nh + hd, rows, :].astype(F32), _cols(p_ref, 2 * nh + nv * hd, nv, rows),
                      g[:, lanes], low[:, lanes], lanes, st_ref.at[hd])
            work.append((ci, rows, hd, h, _gated_intra(h, cum_ref.at[ci], lv)))
    for ci, rows, hd, h, o_intra in work:
        o = _gated_carry(h, o_intra, cum_ref.at[ci])
        gate = _cols(p_ref, 2 * nh + nv * nh + nv * hd, nv, rows).astype(F32)
        o = _rms(o) * gn_ref[...] * _silu(gate)
        o_ref[rows, hd * GLA_DV:(hd + 1) * GLA_DV] = o.astype(o_ref.dtype)


def _gla_scan(p, p_low, w_up, b_gk, g_norm):
    wcat, level = _scan_tables(min(SCAN_CHUNK, p.shape[1]))
    consts = (w_up, b_gk.reshape(1, -1), g_norm.reshape(1, GLA_DV), wcat, level)
    return _scan_call(_gla_kernel, p, p_low, consts, GLA_HEADS, GLA_DK, GLA_DV, "gla_scan")


def _ret_kernel(p_ref, cos_ref, sin_ref, dm_ref, qd_ref, kd_ref, cd_ref, o_ref, st_ref):
    _reset_state(st_ref)
    nh = RET_HEADS
    nv = RET_DV // LANE
    half = RET_DK // 2
    work = []
    for rows in _chunk_rows(p_ref):
        cos = cos_ref[rows, :]
        sin = sin_ref[rows, :]
        qkv = []
        for hd in range(nh):
            q = p_ref[hd, rows, :].astype(F32)
            k = p_ref[nh + hd, rows, :].astype(F32)
            q = q * cos + pltpu.roll(q, half, 1) * sin
            k = (k * cos + pltpu.roll(k, half, 1) * sin) * (RET_DK ** -0.5)
            qkv.append((q, k, _cols(p_ref, 2 * nh + nv * hd, nv, rows)))
        for hd, (q, k, vb) in enumerate(qkv):
            scores = lax.dot_general(q.astype(BF16), k.astype(BF16), _NT,
                                     preferred_element_type=F32) * dm_ref[hd]
            work.append((rows, hd, q, k, vb, jnp.dot(scores.astype(BF16), vb, preferred_element_type=F32)))
    for n in range(0, len(work), nh):
        outs = []
        for rows, hd, q, k, vb, o in work[n:n + nh]:
            st = st_ref[hd]
            o = o + lax.dot_general((q * qd_ref[hd]).astype(BF16), st.astype(BF16), _NT,
                                    preferred_element_type=F32)
            st_ref[hd] = st * cd_ref[hd] + lax.dot_general(vb, (k * kd_ref[hd]).astype(BF16), _TN,
                                                           preferred_element_type=F32)
            outs.append(o)
        for (rows, hd, *_), o in zip(work[n:n + nh], outs):
            o = _rms(o) * _silu(_cols(p_ref, 2 * nh + nv * nh + nv * hd, nv, rows).astype(F32))
            o_ref[rows, hd * RET_DV:(hd + 1) * RET_DV] = o.astype(o_ref.dtype)


def _rope_angles(positions, dim):
    half = dim // 2
    inv_freq = 1.0 / (ROPE_THETA ** (jnp.arange(half, dtype=F32) / half))
    ang = positions.astype(F32)[:, None] * inv_freq
    return jnp.cos(ang), jnp.sin(ang)


def _rope_tables(cos, sin, pad_to):
    pad = jnp.zeros((cos.shape[0], pad_to // 2 - cos.shape[1]), F32)
    cos_t = jnp.concatenate([cos, pad, cos, pad], axis=1)
    sin_t = jnp.concatenate([-sin, pad, sin, pad], axis=1)
    return cos_t, sin_t


def _ret_scan(p, cos, sin):
    nb, s, _ = p.shape
    c = min(SCAN_CHUNK, s)
    h = RET_HEADS
    cos_t, sin_t = _rope_tables(cos, sin, LANE)
    log_gamma = jnp.log(1.0 - 2.0 ** (-5.0 - jnp.arange(h, dtype=F32)))
    idx = jnp.arange(c, dtype=F32)
    causal = jnp.tril(jnp.ones((c, c), dtype=bool))
    lg = log_gamma[:, None, None]
    dm = jnp.exp(jnp.where(causal, lg * (idx[:, None] - idx[None, :]), -jnp.inf))
    ones = jnp.ones((1, 1, LANE), F32)
    qd = jnp.exp(lg * (idx[None, :, None] + 1.0)) * ones
    kd = jnp.exp(lg * (c - 1.0 - idx[None, :, None])) * ones
    cd = jnp.exp(lg * float(c)) * ones
    tm = c * min(SCAN_STEP_CHUNKS, s // c)
    tab = pl.BlockSpec((tm, LANE), lambda t: (t, 0))
    const = lambda a: pl.BlockSpec(a.shape, lambda t: (0,) * a.ndim)
    return pl.pallas_call(
        _ret_kernel,
        grid=(s // tm,),
        in_specs=[pl.BlockSpec((nb, tm, LANE), lambda t: (0, t, 0)), tab, tab,
                  const(dm), const(qd), const(kd), const(cd)],
        out_specs=pl.BlockSpec((tm, h * RET_DV), lambda t: (t, 0)),
        out_shape=jax.ShapeDtypeStruct((s, h * RET_DV), BF16),
        scratch_shapes=[pltpu.VMEM((h, RET_DV, RET_DK), F32)],
        compiler_params=_cparams(1),
        name="ret_scan",
    )(p, cos_t, sin_t, dm, qd, kd, cd)


MLA_QK_PAD = 2 * LANE


def _mla_prep_kernel(h_ref, win_ref, gq_ref, gkv_ref, wuq_ref, wukt_ref, wuv_ref, gqn_ref, gkn_ref,
                     cos_ref, sin_ref, cos_t_ref, sin_t_ref, q_ref, k_ref, v_ref):
    nh = MLA_HEADS
    c = jnp.dot(h_ref[...], win_ref[...], preferred_element_type=F32)
    c_q = _rms(c[:, :MLA_Q_LORA]) * gq_ref[...]
    c_kv = (_rms(c[:, MLA_Q_LORA:MLA_Q_LORA + MLA_KV_LORA]) * gkv_ref[...]).astype(BF16)
    qf = jnp.dot(c_q.astype(BF16), wuq_ref[...], preferred_element_type=F32)
    vf = jnp.dot(c_kv, wuv_ref[...], preferred_element_type=F32)
    knt = lax.dot_general(wukt_ref[...], c_kv, _NT, preferred_element_type=F32)
    krt = c[:, MLA_Q_LORA + MLA_KV_LORA:].T
    kr_ss = jnp.sum(krt * krt, axis=0, keepdims=True)
    cos = cos_ref[...]
    sin = sin_ref[...]
    cos_t = cos_t_ref[...]
    sin_t = sin_t_ref[...]
    gqn = gqn_ref[...]
    gkn = gkn_ref[...]
    half = LANE // 2
    scale = MLA_QK ** -0.5 * math.log2(math.e)
    for hd in range(nh):
        q_n = qf[:, hd * LANE:(hd + 1) * LANE]
        q_r = qf[:, (nh + hd) * LANE:(nh + hd + 1) * LANE]
        ss = jnp.sum(q_n * q_n + q_r * q_r, axis=-1, keepdims=True)
        r = lax.rsqrt(ss * (1.0 / MLA_QK) + NORM_EPS)
        q_n = q_n * r * gqn[:, :LANE]
        q_r = q_r * r * gqn[:, LANE:]
        q_r = q_r * cos + pltpu.roll(q_r, half, 1) * sin
        q_ref[hd] = (jnp.concatenate([q_n, q_r], axis=1) * scale).astype(q_ref.dtype)
        k_n = knt[hd * LANE:(hd + 1) * LANE]
        ss = jnp.sum(k_n * k_n, axis=0, keepdims=True) + kr_ss
        r = lax.rsqrt(ss * (1.0 / MLA_QK) + NORM_EPS)
        k_n = k_n * r * gkn[:LANE]
        k_r = krt * r * gkn[LANE:]
        k_r = k_r * cos_t + jnp.concatenate([k_r[half:], k_r[:half]], axis=0) * sin_t
        k_ref[hd] = jnp.concatenate([k_n, k_r], axis=0).astype(k_ref.dtype)
        v_h = vf[:, hd * LANE:(hd + 1) * LANE]
        v_ref[hd] = jnp.concatenate([v_h, jnp.ones_like(v_h)], axis=1).astype(v_ref.dtype)


def _pad_rope_cols(w):
    half = MLA_ROPE // 2
    z = jnp.zeros(w.shape[:-1] + (LANE // 2 - half,), w.dtype)
    return jnp.concatenate([w[..., :half], z, w[..., half:], z], axis=-1)


def _mla_prep(h, cos, sin, w_in, g_q_lora, g_kv_lora, w_uq, w_ukv, g_qnorm, g_knorm):
    s = h.shape[0]
    tm = min(MLA_TM, s)
    nh = MLA_HEADS
    lat = MLA_Q_LORA + MLA_KV_LORA
    w_in_p = jnp.concatenate([w_in[:, :lat], _pad_rope_cols(w_in[:, lat:])], axis=1).astype(BF16)
    wq = w_uq.reshape(MLA_Q_LORA, nh, MLA_QK)
    wq_nope = wq[:, :, :MLA_NOPE].reshape(MLA_Q_LORA, nh * LANE)
    wq_rope = _pad_rope_cols(wq[:, :, MLA_NOPE:]).reshape(MLA_Q_LORA, nh * LANE)
    w_uq_p = jnp.concatenate([wq_nope, wq_rope], axis=1).astype(BF16)
    wkv = w_ukv.reshape(MLA_KV_LORA, nh, MLA_NOPE + MLA_V)
    w_uk_t = wkv[:, :, :MLA_NOPE].reshape(MLA_KV_LORA, nh * LANE).T.astype(BF16)
    w_uv = wkv[:, :, MLA_NOPE:].reshape(MLA_KV_LORA, nh * LANE).astype(BF16)
    pad_gain = lambda g: jnp.concatenate([g[:MLA_NOPE], _pad_rope_cols(g[MLA_NOPE:])])
    cos_t, sin_t = _rope_tables(cos, sin, LANE)
    const = lambda a: pl.BlockSpec(a.shape, lambda i: (0,) * a.ndim)
    args = (w_in_p, g_q_lora.reshape(1, -1), g_kv_lora.reshape(1, -1), w_uq_p, w_uk_t, w_uv,
            pad_gain(g_qnorm).reshape(1, -1), pad_gain(g_knorm).reshape(-1, 1))
    tab = pl.BlockSpec((tm, LANE), lambda i: (i, 0))
    tab_t = pl.BlockSpec((LANE, tm), lambda i: (0, i))
    return pl.pallas_call(
        _mla_prep_kernel,
        grid=(s // tm,),
        in_specs=([pl.BlockSpec((tm, D_MODEL), lambda i: (i, 0))] + [const(a) for a in args]
                  + [tab, tab, tab_t, tab_t]),
        out_specs=[pl.BlockSpec((nh, tm, MLA_QK_PAD), lambda i: (0, i, 0)),
                   pl.BlockSpec((nh, MLA_QK_PAD, tm), lambda i: (0, 0, i)),
                   pl.BlockSpec((nh, tm, 2 * MLA_V), lambda i: (0, i, 0))],
        out_shape=[jax.ShapeDtypeStruct((nh, s, MLA_QK_PAD), BF16),
                   jax.ShapeDtypeStruct((nh, MLA_QK_PAD, s), BF16),
                   jax.ShapeDtypeStruct((nh, s, 2 * MLA_V), BF16)],
        compiler_params=_cparams(1),
        name="mla_prep",
    )(h, *args, cos_t, sin_t, cos_t.T, sin_t.T)


def _flash_kernel(q_ref, qn_ref, k_ref, v_ref, o_ref, s_ref, m_ref, acc_ref, *, sub, tk):
    r = sub // tk
    i = pl.program_id(1)
    m_ref[...] = jnp.full(m_ref.shape, -jnp.inf, F32)
    acc_ref[...] = jnp.zeros(acc_ref.shape, F32)

    def scores(u, j, dst, src_ref=q_ref):
        off = pl.multiple_of(j * tk, tk)
        s_ref[dst, u] = jnp.dot(src_ref[0, u * sub:(u + 1) * sub, :], k_ref[0, :, pl.ds(off, tk)],
                                preferred_element_type=F32)

    def accumulate(u, j, src, shift):
        off = pl.multiple_of(j * tk, tk)
        s = s_ref[src, u]
        if shift is not None:
            row = lax.broadcasted_iota(jnp.int32, (sub, tk), 0)
            col = lax.broadcasted_iota(jnp.int32, (sub, tk), 1)
            s = jnp.where(col - row <= shift, s, -jnp.inf)
        m_prev = m_ref[u]
        m_new = jnp.maximum(m_prev, jnp.max(s, axis=-1, keepdims=True))
        p = jnp.exp2(s - m_new).astype(BF16)
        pv = jnp.dot(p, v_ref[0, pl.ds(off, tk), :], preferred_element_type=F32)
        acc_ref[u] = jnp.exp2(m_prev - m_new) * acc_ref[u] + pv
        m_ref[u] = m_new

    @pl.when(i == 0)
    def _():
        for u in range(2):
            scores(u, 0, 0)

    def body(jj, carry):
        for u in range(2):
            scores(u, 2 * jj + 1, 1)
        for u in range(2):
            accumulate(u, 2 * jj, 0, None)
        for u in range(2):
            scores(u, 2 * jj + 2, 0)
        for u in range(2):
            accumulate(u, 2 * jj + 1, 1, None)
        return carry

    lax.fori_loop(0, r * i, body, 0)
    base = 2 * r * i
    for d in range(2 * r):
        if d + 1 < 2 * r:
            for u in range(2):
                if d + 1 < r * (u + 1):
                    scores(u, base + d + 1, (d + 1) % 2)
        else:
            for u in range(2):
                scores(u, 0, 0, qn_ref)
        for u in range(2):
            if d < r * (u + 1):
                accumulate(u, base + d, d % 2, None if d < r * u else (r * u - d) * tk)
    for u in range(2):
        acc = acc_ref[u]
        o_ref[u * sub:(u + 1) * sub, :] = (acc[:, :MLA_V] / acc[:, MLA_V:]).astype(o_ref.dtype)


def _flash(q, k, v):
    nh, s, _ = q.shape
    sub = min(ATTN_SUB, s // 2)
    tk = min(ATTN_TK, sub)
    tq = 2 * sub
    n_blocks = s // tq
    q_spec = lambda index: pl.BlockSpec((1, tq, MLA_QK_PAD), index)
    return pl.pallas_call(
        functools.partial(_flash_kernel, sub=sub, tk=tk),
        grid=(nh, n_blocks),
        in_specs=[q_spec(lambda h, i: (h, i, 0)),
                  q_spec(lambda h, i: (h, jnp.minimum(i + 1, n_blocks - 1), 0)),
                  pl.BlockSpec((1, MLA_QK_PAD, s), lambda h, i: (h, 0, 0)),
                  pl.BlockSpec((1, s, 2 * MLA_V), lambda h, i: (h, 0, 0))],
        out_specs=pl.BlockSpec((tq, MLA_V), lambda h, i: (i, h)),
        out_shape=jax.ShapeDtypeStruct((s, nh * MLA_V), BF16),
        scratch_shapes=[pltpu.VMEM((2, 2, sub, tk), F32),
                        pltpu.VMEM((2, sub, 1), F32),
                        pltpu.VMEM((2, sub, 2 * MLA_V), F32)],
        compiler_params=_cparams(2),
        name="flash",
    )(q, q, k, v)


def _out_ffn_kernel(a_ref, x_ref, wo_ref, gf_ref, wgu_ref, wd_ref, gn_ref, xo_ref, *maybe_ho_ref):
    x1 = x_ref[...] + jnp.dot(a_ref[...], wo_ref[...], preferred_element_type=F32)
    h = (_rms(x1) * gf_ref[...]).astype(BF16)
    au = jnp.dot(h, wgu_ref[...], preferred_element_type=F32)
    p = (_silu(au[:, :FFN_HIDDEN]) * au[:, FFN_HIDDEN:]).astype(BF16)
    x2 = x1 + jnp.dot(p, wd_ref[...], preferred_element_type=F32)
    xo_ref[...] = x2
    for ho_ref in maybe_ho_ref:
        ho_ref[...] = (_rms(x2) * gn_ref[...]).astype(ho_ref.dtype)


def _out_ffn(a, x, w_out, g_ffn, w_gate_up, w_down, layer, g_next, emit_next):
    s, d = x.shape
    tm = min(FFN_TM, s)
    row = lambda width: pl.BlockSpec((tm, width), lambda i: (i, 0))
    resident = lambda arr: pl.BlockSpec(arr.shape, lambda i: (0, 0), pipeline_mode=pl.Buffered(1))
    of_layer = lambda arr: pl.BlockSpec((None,) + arr.shape[1:], lambda i: (layer, 0, 0),
                                        pipeline_mode=pl.Buffered(1))
    gf = g_ffn.reshape(1, d)
    gn = g_next.reshape(1, d)
    n_out = 2 if emit_next else 1
    return pl.pallas_call(
        _out_ffn_kernel,
        grid=(s // tm,),
        in_specs=[row(a.shape[1]), row(d), resident(w_out), resident(gf), of_layer(w_gate_up),
                  of_layer(w_down), resident(gn)],
        out_specs=[row(d), row(d)][:n_out],
        out_shape=[jax.ShapeDtypeStruct((s, d), F32), jax.ShapeDtypeStruct((s, d), BF16)][:n_out],
        compiler_params=_cparams(1),
        name="out_ffn",
    )(a, x, w_out, gf, w_gate_up, w_down, gn)


def kernel(x, positions, norm_mix, norm_ffn, hgrn_w_in, hgrn_g_norm, hgrn_w_out, hgrn_lb_logits,
           gla_w_in, gla_w_gk_up, gla_b_gk, gla_g_norm, gla_w_out, ret_w_in, ret_w_out, mla_w_in,
           mla_g_q_lora, mla_g_kv_lora, mla_w_uq, mla_w_ukv, mla_g_qnorm, mla_g_knorm, mla_w_out,
           ffn_w_gate_up, ffn_w_down):
    b, s, d = x.shape
    depth = norm_mix.shape[0]
    lower_bounds = jnp.cumsum(jax.nn.softmax(hgrn_lb_logits.astype(F32), axis=0), axis=0)
    w_gate_up = ffn_w_gate_up.astype(BF16)
    w_down = ffn_w_down.astype(BF16)
    outs = []
    for bi in range(b):
        xs = x[bi]
        pos = positions[bi]
        cos, sin = _rope_angles(pos, RET_DK)
        step = RET_DK // MLA_ROPE
        h = _rmsnorm(xs, norm_mix[0])
        for i in range(depth):
            mixer, j = i % 4, i // 4
            if mixer == 0:
                w = hgrn_w_in[j]
                p = _proj(h, w, 3, lambda t: t + jnp.minimum(t, 1), BF16)
                pf = _proj(h, w, 1, lambda t: t + 1, F32)
                a = _hgrn_scan(p, pf, lower_bounds[i], hgrn_g_norm[j])
                w_out = hgrn_w_out[j]
            elif mixer == 1:
                main = 2 * GLA_KEY_DIM + 2 * GLA_VALUE_DIM
                pad = LANE - GLA_GATE_RANK
                p = _proj(h, gla_w_in[j], main // PROJ_TN, lambda t: t, BF16)
                w_low = jnp.pad(gla_w_in[j][:, main:], ((0, 0), (0, pad)))
                p_low = _proj(h, w_low, 1, lambda t: t, F32)
                w_up = jnp.pad(gla_w_gk_up[j], ((0, pad), (0, 0))).astype(BF16)
                a = _gla_scan(p, p_low, w_up, gla_b_gk[j], gla_g_norm[j])
                w_out = gla_w_out[j]
            elif mixer == 2:
                p = _proj(h, ret_w_in[j], ret_w_in.shape[2] // PROJ_TN, lambda t: t, BF16)
                a = _ret_scan(p, cos, sin)
                w_out = ret_w_out[j]
            else:
                q, k, v = _mla_prep(h, cos[:, ::step], sin[:, ::step], mla_w_in[j], mla_g_q_lora[j], mla_g_kv_lora[j],
                                    mla_w_uq[j], mla_w_ukv[j], mla_g_qnorm[j], mla_g_knorm[j])
                a = _flash(q, k, v)
                w_out = mla_w_out[j]
            last = i + 1 == depth
            res = _out_ffn(a, xs, w_out.astype(BF16), norm_ffn[i], w_gate_up, w_down, i,
                           norm_mix[0 if last else i + 1], not last)
            xs = res[0]
            h = None if last else res[1]
        outs.append(xs)
    return outs[0][None] if b == 1 else jnp.stack(outs, axis=0)
```

```python
import functools
import math
from typing import Any, NamedTuple

import numpy as np
import jax
import jax.numpy as jnp
from jax import lax
from jax.experimental import pallas as pl
from jax.experimental.pallas import tpu as pltpu

F32 = jnp.float32
BF16 = jnp.bfloat16

D_MODEL = 1024
NORM_EPS = 1e-6
ROPE_THETA = 10000.0
HGRN_HEADS, HGRN_DK, HGRN_DV = 8, 128, 128
GLA_HEADS, GLA_DK, GLA_DV = 4, 128, 256
GLA_KEY_DIM, GLA_VALUE_DIM, GLA_GATE_RANK = 512, 1024, 16
GLA_GATE_NORMALIZER = 16.0
RET_HEADS, RET_DK, RET_DV = 8, 128, 256
MLA_HEADS, MLA_Q_LORA, MLA_KV_LORA = 8, 384, 128
MLA_NOPE, MLA_ROPE, MLA_V = 128, 64, 128
MLA_QK = MLA_NOPE + MLA_ROPE
FFN_HIDDEN = 2816

LANE = 128
VMEM_LIMIT_BYTES = 56 * 1024 * 1024

SCAN_CHUNK = 128
SCAN_STEP_CHUNKS = 4
PROJ_TM, PROJ_TN = 2048, 1024
FFN_TM = 512
FFN_ROW_GROUPS = 2
MLA_TM = 256
ATTN_TK = 512
ATTN_SUB = 512

_NT = (((1,), (1,)), ((), ()))
_TN = (((0,), (0,)), ((), ()))


def _cparams(n_axes):
    return pltpu.CompilerParams(dimension_semantics=("arbitrary",) * n_axes,
                                vmem_limit_bytes=VMEM_LIMIT_BYTES)


def _rms(x, width=None):
    width = x.shape[-1] if width is None else width
    ss = jnp.sum(x * x, axis=-1, keepdims=True)
    return x * lax.rsqrt(ss * (1.0 / width) + NORM_EPS)


def _silu(x):
    return x * jax.nn.sigmoid(x)


def _rmsnorm_kernel(x_ref, g_ref, o_ref):
    o_ref[...] = (_rms(x_ref[...]) * g_ref[...]).astype(o_ref.dtype)


def _rmsnorm(x, gain, tm=1024):
    s, d = x.shape
    tm = min(tm, s)
    return pl.pallas_call(
        _rmsnorm_kernel,
        grid=(s // tm,),
        in_specs=[pl.BlockSpec((tm, d), lambda i: (i, 0)),
                  pl.BlockSpec((1, d), lambda i: (0, 0))],
        out_specs=pl.BlockSpec((tm, d), lambda i: (i, 0)),
        out_shape=jax.ShapeDtypeStruct((s, d), BF16),
        compiler_params=_cparams(1),
        name="rmsnorm",
    )(x, gain.reshape(1, d))


def _proj_kernel(h_ref, w_ref, o_ref):
    acc = jnp.dot(h_ref[...], w_ref[...].astype(BF16), preferred_element_type=F32)
    for c in range(acc.shape[1] // LANE):
        o_ref[c] = acc[:, c * LANE:(c + 1) * LANE].astype(o_ref.dtype)


def _proj(h, w, n_tiles, col_tile, out_dtype):
    s, k = h.shape
    tm = min(PROJ_TM, s)
    tn = min(PROJ_TN, w.shape[1])
    return pl.pallas_call(
        _proj_kernel,
        grid=(n_tiles, s // tm),
        in_specs=[pl.BlockSpec((tm, k), lambda j, i: (i, 0)),
                  pl.BlockSpec((k, tn), lambda j, i: (0, col_tile(j)))],
        out_specs=pl.BlockSpec((tn // LANE, tm, LANE), lambda j, i: (j, i, 0)),
        out_shape=jax.ShapeDtypeStruct((n_tiles * tn // LANE, s, LANE), out_dtype),
        compiler_params=_cparams(2),
        name="proj",
    )(h, w)


N_MATMUL_LEVELS = 3


def _scan_tables(chunk):
    n_levels = int(math.log2(chunk))
    t = np.arange(chunk)[:, None]
    r = np.arange(chunk)[None, :]
    blocks = [r <= t]
    level = np.full((chunk, chunk), -1, np.int32)
    level[np.arange(chunk), np.arange(chunk)] = n_levels
    for l in range(n_levels):
        m = 1 << l
        b = (t // (2 * m)) * (2 * m) + m - 1
        if 1 <= l < N_MATMUL_LEVELS:
            blocks.append(np.where(t > b, (r > b) & (r <= t), (r > t) & (r <= b)))
        same = (t // (2 * m)) == (r // (2 * m))
        level[same & (t % (2 * m) >= m) & (r % (2 * m) < m)] = l
    w = np.concatenate(blocks, axis=0).astype(np.float32)
    wcat = np.concatenate([w, w], axis=1)
    return jnp.asarray(wcat, BF16), jnp.asarray(level)


def _gate_sums(g, wcat_ref, cum_ref):
    c = g.shape[0]
    g_hi = g.astype(BF16)
    g_lo = (g - g_hi.astype(F32)).astype(BF16)
    gcat = jnp.concatenate([g_hi, g_lo], axis=0)
    sums = jnp.dot(wcat_ref[...], gcat, preferred_element_type=F32)
    cum_ref[...] = sums[:c]
    return sums[c:]


def _level_exponent(l, g, low, cum_ref, lanes):
    c = g.shape[0]
    if l == 0:
        odd = lax.broadcasted_iota(jnp.int32, g.shape, 0) % 2 == 1
        return jnp.where(odd, g, 0.0)
    if l < N_MATMUL_LEVELS:
        return low[(l - 1) * c:l * c]
    m = 1 << l
    parts = []
    for a in range(0, c, 2 * m):
        edge = cum_ref[a + m - 1:a + m, lanes]
        parts.append(edge - cum_ref[a:a + m, lanes])
        parts.append(cum_ref[a + m:a + 2 * m, lanes] - edge)
    return jnp.concatenate(parts, axis=0)


class _Head(NamedTuple):
    q: jax.Array
    k: jax.Array
    vb: jax.Array
    g: jax.Array
    low: jax.Array
    lanes: slice
    st_ref: Any


def _gated_intra(h, cum_ref, lv):
    c = h.q.shape[0]
    n_levels = int(math.log2(c))
    scores = jnp.where(lv == n_levels,
                       lax.dot_general(h.q.astype(BF16), h.k.astype(BF16), _NT,
                                       preferred_element_type=F32), 0.0)
    for l in range(n_levels):
        e = jnp.exp2(_level_exponent(l, h.g, h.low, cum_ref, h.lanes))
        s = jnp.dot((h.q * e).astype(BF16), (h.k * e).T.astype(BF16), preferred_element_type=F32)
        scores = jnp.where(lv == l, s, scores)
    return jnp.dot(scores.astype(BF16), h.vb, preferred_element_type=F32)


def _gated_carry(h, o_intra, cum_ref):
    c = h.q.shape[0]
    cum = cum_ref[:, h.lanes]
    last = cum_ref[c - 1:c, h.lanes]
    q_in = (h.q * jnp.exp2(cum)).astype(BF16)
    k_out = (h.k * jnp.exp2(last - cum)).astype(BF16)
    st = h.st_ref[...]
    o = o_intra + lax.dot_general(q_in, st.astype(BF16), _NT, preferred_element_type=F32)
    h.st_ref[...] = st * jnp.exp2(last) + lax.dot_general(h.vb, k_out, _TN, preferred_element_type=F32)
    return o


def _reset_state(st_ref):
    @pl.when(pl.program_id(0) == 0)
    def _():
        st_ref[...] = jnp.zeros_like(st_ref)


def _cols(p_ref, first, n, rows):
    if n == 1:
        return p_ref[first, rows, :]
    return jnp.concatenate([p_ref[first + i, rows, :] for i in range(n)], axis=1)


def _chunk_rows(ref):
    return [slice(r, r + SCAN_CHUNK) for r in range(0, ref.shape[1], SCAN_CHUNK)]


def _hgrn_kernel(p_ref, f_ref, lb_ref, gn_ref, wcat_ref, lv_ref, o_ref, st_ref, cum_ref):
    _reset_state(st_ref)
    nh = HGRN_HEADS
    lb = lb_ref[...]
    lv = lv_ref[...]
    work = []
    for ci, rows in enumerate(_chunk_rows(p_ref)):
        forget = lb + (1.0 - lb) * jax.nn.sigmoid(_cols(f_ref, 0, nh, rows))
        g = jnp.log2(forget)
        low = _gate_sums(g, wcat_ref, cum_ref.at[ci])
        for hd in range(nh):
            lanes = slice(hd * LANE, (hd + 1) * LANE)
            q = _silu(p_ref[hd, rows, :].astype(F32)) * (HGRN_DK ** -0.5)
            h = _Head(q, 1.0 - forget[:, lanes], p_ref[nh + hd, rows, :], g[:, lanes], low[:, lanes],
                      lanes, st_ref.at[hd])
            work.append((ci, rows, hd, h, _gated_intra(h, cum_ref.at[ci], lv)))
    for ci, rows, hd, h, o_intra in work:
        o = _gated_carry(h, o_intra, cum_ref.at[ci])
        o = _rms(o) * gn_ref[...] * _silu(p_ref[2 * nh + hd, rows, :].astype(F32))
        o_ref[rows, hd * LANE:(hd + 1) * LANE] = o.astype(o_ref.dtype)


def _scan_call(body, p, pg, consts, n_heads, dk, dv, name):
    s = p.shape[1]
    c = min(SCAN_CHUNK, s)
    n_chunks = min(SCAN_STEP_CHUNKS, s // c)
    tm = c * n_chunks
    const = lambda a: pl.BlockSpec(a.shape, lambda t: (0,) * a.ndim)
    blocks = lambda a: pl.BlockSpec((a.shape[0], tm, LANE), lambda t: (0, t, 0))
    return pl.pallas_call(
        body,
        grid=(s // tm,),
        in_specs=[blocks(p), blocks(pg)] + [const(a) for a in consts],
        out_specs=pl.BlockSpec((tm, n_heads * dv), lambda t: (t, 0)),
        out_shape=jax.ShapeDtypeStruct((s, n_heads * dv), BF16),
        scratch_shapes=[pltpu.VMEM((n_heads, dv, dk), F32),
                        pltpu.VMEM((n_chunks, c, n_heads * dk), F32)],
        compiler_params=_cparams(1),
        name=name,
    )(p, pg, *consts)


def _hgrn_scan(p, pf, lower_bound, g_norm):
    wcat, level = _scan_tables(min(SCAN_CHUNK, p.shape[1]))
    consts = (lower_bound.reshape(1, -1), g_norm.reshape(1, HGRN_DV), wcat, level)
    return _scan_call(_hgrn_kernel, p, pf, consts, HGRN_HEADS, HGRN_DK, HGRN_DV, "hgrn_scan")


def _log_sigmoid(z):
    return jnp.minimum(z, 0.0) - jnp.log(1.0 + jnp.exp(-jnp.abs(z)))


def _gla_kernel(p_ref, low_ref, wup_ref, b_ref, gn_ref, wcat_ref, lv_ref, o_ref, st_ref, cum_ref):
    _reset_state(st_ref)
    nh = GLA_HEADS
    nv = GLA_DV // LANE
    lv = lv_ref[...]
    work = []
    for ci, rows in enumerate(_chunk_rows(p_ref)):
        z = jnp.dot(low_ref[0, rows, :].astype(BF16), wup_ref[...], preferred_element_type=F32) + b_ref[...]
        g = _log_sigmoid(z) * (math.log2(math.e) / GLA_GATE_NORMALIZER)
        low = _gate_sums(g, wcat_ref, cum_ref.at[ci])
        for hd in range(nh):
            lanes = slice(hd * LANE, (hd + 1) * LANE)
            q = p_ref[hd, rows, :].astype(F32) * (GLA_DK ** -0.5)
            h = _Head(q, p_ref[nh + hd, rows, :].astype(F32), _cols(p_ref, 2 * nh + nv * hd, nv, rows),
                      g[:, lanes], low[:, lanes], lanes, st_ref.at[hd])
            work.append((ci, rows, hd, h, _gated_intra(h, cum_ref.at[ci], lv)))
    for ci, rows, hd, h, o_intra in work:
        o = _gated_carry(h, o_intra, cum_ref.at[ci])
        gate = _cols(p_ref, 2 * nh + nv * nh + nv * hd, nv, rows).astype(F32)
        o = _rms(o) * gn_ref[...] * _silu(gate)
        o_ref[rows, hd * GLA_DV:(hd + 1) * GLA_DV] = o.astype(o_ref.dtype)


def _gla_scan(p, p_low, w_up, b_gk, g_norm):
    wcat, level = _scan_tables(min(SCAN_CHUNK, p.shape[1]))
    consts = (w_up, b_gk.reshape(1, -1), g_norm.reshape(1, GLA_DV), wcat, level)
    return _scan_call(_gla_kernel, p, p_low, consts, GLA_HEADS, GLA_DK, GLA_DV, "gla_scan")


def _ret_kernel(p_ref, cos_ref, sin_ref, dm_ref, qd_ref, kd_ref, cd_ref, o_ref, st_ref):
    _reset_state(st_ref)
    nh = RET_HEADS
    nv = RET_DV // LANE
    half = RET_DK // 2
    work = []
    for rows in _chunk_rows(p_ref):
        cos = cos_ref[rows, :]
        sin = sin_ref[rows, :]
        qkv = []
        for hd in range(nh):
            q = p_ref[hd, rows, :].astype(F32)
            k = p_ref[nh + hd, rows, :].astype(F32)
            q = q * cos + pltpu.roll(q, half, 1) * sin
            k = (k * cos + pltpu.roll(k, half, 1) * sin) * (RET_DK ** -0.5)
            qkv.append((q, k, _cols(p_ref, 2 * nh + nv * hd, nv, rows)))
        for hd, (q, k, vb) in enumerate(qkv):
            scores = lax.dot_general(q.astype(BF16), k.astype(BF16), _NT,
                                     preferred_element_type=F32) * dm_ref[hd]
            work.append((rows, hd, q, k, vb, jnp.dot(scores.astype(BF16), vb, preferred_element_type=F32)))
    for n in range(0, len(work), nh):
        outs = []
        for rows, hd, q, k, vb, o in work[n:n + nh]:
            st = st_ref[hd]
            o = o + lax.dot_general((q * qd_ref[hd]).astype(BF16), st.astype(BF16), _NT,
                                    preferred_element_type=F32)
            st_ref[hd] = st * cd_ref[hd] + lax.dot_general(vb, (k * kd_ref[hd]).astype(BF16), _TN,
                                                           preferred_element_type=F32)
            outs.append(o)
        for (rows, hd, *_), o in zip(work[n:n + nh], outs):
            o = _rms(o) * _silu(_cols(p_ref, 2 * nh + nv * nh + nv * hd, nv, rows).astype(F32))
            o_ref[rows, hd * RET_DV:(hd + 1) * RET_DV] = o.astype(o_ref.dtype)


def _rope_angles(positions, dim):
    half = dim // 2
    inv_freq = 1.0 / (ROPE_THETA ** (jnp.arange(half, dtype=F32) / half))
    ang = positions.astype(F32)[:, None] * inv_freq
    return jnp.cos(ang), jnp.sin(ang)


def _rope_tables(cos, sin, pad_to):
    pad = jnp.zeros((cos.shape[0], pad_to // 2 - cos.shape[1]), F32)
    cos_t = jnp.concatenate([cos, pad, cos, pad], axis=1)
    sin_t = jnp.concatenate([-sin, pad, sin, pad], axis=1)
    return cos_t, sin_t


def _ret_scan(p, cos, sin):
    nb, s, _ = p.shape
    c = min(SCAN_CHUNK, s)
    h = RET_HEADS
    cos_t, sin_t = _rope_tables(cos, sin, LANE)
    log_gamma = jnp.log(1.0 - 2.0 ** (-5.0 - jnp.arange(h, dtype=F32)))
    idx = jnp.arange(c, dtype=F32)
    causal = jnp.tril(jnp.ones((c, c), dtype=bool))
    lg = log_gamma[:, None, None]
    dm = jnp.exp(jnp.where(causal, lg * (idx[:, None] - idx[None, :]), -jnp.inf))
    ones = jnp.ones((1, 1, LANE), F32)
    qd = jnp.exp(lg * (idx[None, :, None] + 1.0)) * ones
    kd = jnp.exp(lg * (c - 1.0 - idx[None, :, None])) * ones
    cd = jnp.exp(lg * float(c)) * ones
    tm = c * min(SCAN_STEP_CHUNKS, s // c)
    tab = pl.BlockSpec((tm, LANE), lambda t: (t, 0))
    const = lambda a: pl.BlockSpec(a.shape, lambda t: (0,) * a.ndim)
    return pl.pallas_call(
        _ret_kernel,
        grid=(s // tm,),
        in_specs=[pl.BlockSpec((nb, tm, LANE), lambda t: (0, t, 0)), tab, tab,
                  const(dm), const(qd), const(kd), const(cd)],
        out_specs=pl.BlockSpec((tm, h * RET_DV), lambda t: (t, 0)),
        out_shape=jax.ShapeDtypeStruct((s, h * RET_DV), BF16),
        scratch_shapes=[pltpu.VMEM((h, RET_DV, RET_DK), F32)],
        compiler_params=_cparams(1),
        name="ret_scan",
    )(p, cos_t, sin_t, dm, qd, kd, cd)


MLA_QK_PAD = 2 * LANE


def _mla_prep_kernel(h_ref, win_ref, gq_ref, gkv_ref, wuq_ref, wukt_ref, wuv_ref, gqn_ref, gkn_ref,
                     cos_ref, sin_ref, cos_t_ref, sin_t_ref, q_ref, k_ref, v_ref):
    nh = MLA_HEADS
    c = jnp.dot(h_ref[...], win_ref[...], preferred_element_type=F32)
    c_q = _rms(c[:, :MLA_Q_LORA]) * gq_ref[...]
    c_kv = (_rms(c[:, MLA_Q_LORA:MLA_Q_LORA + MLA_KV_LORA]) * gkv_ref[...]).astype(BF16)
    qf = jnp.dot(c_q.astype(BF16), wuq_ref[...], preferred_element_type=F32)
    vf = jnp.dot(c_kv, wuv_ref[...], preferred_element_type=F32)
    knt = lax.dot_general(wukt_ref[...], c_kv, _NT, preferred_element_type=F32)
    krt = c[:, MLA_Q_LORA + MLA_KV_LORA:].T
    kr_ss = jnp.sum(krt * krt, axis=0, keepdims=True)
    cos = cos_ref[...]
    sin = sin_ref[...]
    cos_t = cos_t_ref[...]
    sin_t = sin_t_ref[...]
    gqn = gqn_ref[...]
    gkn = gkn_ref[...]
    half = LANE // 2
    scale = MLA_QK ** -0.5 * math.log2(math.e)
    for hd in range(nh):
        q_n = qf[:, hd * LANE:(hd + 1) * LANE]
        q_r = qf[:, (nh + hd) * LANE:(nh + hd + 1) * LANE]
        ss = jnp.sum(q_n * q_n + q_r * q_r, axis=-1, keepdims=True)
        r = lax.rsqrt(ss * (1.0 / MLA_QK) + NORM_EPS)
        q_n = q_n * r * gqn[:, :LANE]
        q_r = q_r * r * gqn[:, LANE:]
        q_r = q_r * cos + pltpu.roll(q_r, half, 1) * sin
        q_ref[hd] = (jnp.concatenate([q_n, q_r], axis=1) * scale).astype(q_ref.dtype)
        k_n = knt[hd * LANE:(hd + 1) * LANE]
        ss = jnp.sum(k_n * k_n, axis=0, keepdims=True) + kr_ss
        r = lax.rsqrt(ss * (1.0 / MLA_QK) + NORM_EPS)
        k_n = k_n * r * gkn[:LANE]
        k_r = krt * r * gkn[LANE:]
        k_r = k_r * cos_t + jnp.concatenate([k_r[half:], k_r[:half]], axis=0) * sin_t
        k_ref[hd] = jnp.concatenate([k_n, k_r], axis=0).astype(k_ref.dtype)
        v_h = vf[:, hd * LANE:(hd + 1) * LANE]
        v_ref[hd] = jnp.concatenate([v_h, jnp.ones_like(v_h)], axis=1).astype(v_ref.dtype)


def _pad_rope_cols(w):
    half = MLA_ROPE // 2
    z = jnp.zeros(w.shape[:-1] + (LANE // 2 - half,), w.dtype)
    return jnp.concatenate([w[..., :half], z, w[..., half:], z], axis=-1)


def _mla_prep(h, cos, sin, w_in, g_q_lora, g_kv_lora, w_uq, w_ukv, g_qnorm, g_knorm):
    s = h.shape[0]
    tm = min(MLA_TM, s)
    nh = MLA_HEADS
    lat = MLA_Q_LORA + MLA_KV_LORA
    w_in_p = jnp.concatenate([w_in[:, :lat], _pad_rope_cols(w_in[:, lat:])], axis=1).astype(BF16)
    wq = w_uq.reshape(MLA_Q_LORA, nh, MLA_QK)
    wq_nope = wq[:, :, :MLA_NOPE].reshape(MLA_Q_LORA, nh * LANE)
    wq_rope = _pad_rope_cols(wq[:, :, MLA_NOPE:]).reshape(MLA_Q_LORA, nh * LANE)
    w_uq_p = jnp.concatenate([wq_nope, wq_rope], axis=1).astype(BF16)
    wkv = w_ukv.reshape(MLA_KV_LORA, nh, MLA_NOPE + MLA_V)
    w_uk_t = wkv[:, :, :MLA_NOPE].reshape(MLA_KV_LORA, nh * LANE).T.astype(BF16)
    w_uv = wkv[:, :, MLA_NOPE:].reshape(MLA_KV_LORA, nh * LANE).astype(BF16)
    pad_gain = lambda g: jnp.concatenate([g[:MLA_NOPE], _pad_rope_cols(g[MLA_NOPE:])])
    cos_t, sin_t = _rope_tables(cos, sin, LANE)
    const = lambda a: pl.BlockSpec(a.shape, lambda i: (0,) * a.ndim)
    args = (w_in_p, g_q_lora.reshape(1, -1), g_kv_lora.reshape(1, -1), w_uq_p, w_uk_t, w_uv,
            pad_gain(g_qnorm).reshape(1, -1), pad_gain(g_knorm).reshape(-1, 1))
    tab = pl.BlockSpec((tm, LANE), lambda i: (i, 0))
    tab_t = pl.BlockSpec((LANE, tm), lambda i: (0, i))
    return pl.pallas_call(
        _mla_prep_kernel,
        grid=(s // tm,),
        in_specs=([pl.BlockSpec((tm, D_MODEL), lambda i: (i, 0))] + [const(a) for a in args]
                  + [tab, tab, tab_t, tab_t]),
        out_specs=[pl.BlockSpec((nh, tm, MLA_QK_PAD), lambda i: (0, i, 0)),
                   pl.BlockSpec((nh, MLA_QK_PAD, tm), lambda i: (0, 0, i)),
                   pl.BlockSpec((nh, tm, 2 * MLA_V), lambda i: (0, i, 0))],
        out_shape=[jax.ShapeDtypeStruct((nh, s, MLA_QK_PAD), BF16),
                   jax.ShapeDtypeStruct((nh, MLA_QK_PAD, s), BF16),
                   jax.ShapeDtypeStruct((nh, s, 2 * MLA_V), BF16)],
        compiler_params=_cparams(1),
        name="mla_prep",
    )(h, *args, cos_t, sin_t, cos_t.T, sin_t.T)


def _flash_kernel(q_ref, qn_ref, k_ref, v_ref, o_ref, s_ref, m_ref, acc_ref, *, sub, tk):
    r = sub // tk
    i = pl.program_id(1)
    m_ref[...] = jnp.full(m_ref.shape, -jnp.inf, F32)
    acc_ref[...] = jnp.zeros(acc_ref.shape, F32)

    def scores(u, j, dst, src_ref=q_ref):
        off = pl.multiple_of(j * tk, tk)
        s_ref[dst, u] = jnp.dot(src_ref[0, u * sub:(u + 1) * sub, :], k_ref[0, :, pl.ds(off, tk)],
                                preferred_element_type=F32)

    def accumulate(u, j, src, shift):
        off = pl.multiple_of(j * tk, tk)
        s = s_ref[src, u]
        if shift is not None:
            row = lax.broadcasted_iota(jnp.int32, (sub, tk), 0)
            col = lax.broadcasted_iota(jnp.int32, (sub, tk), 1)
            s = jnp.where(col - row <= shift, s, -jnp.inf)
        m_prev = m_ref[u]
        m_new = jnp.maximum(m_prev, jnp.max(s, axis=-1, keepdims=True))
        p = jnp.exp2(s - m_new).astype(BF16)
        pv = jnp.dot(p, v_ref[0, pl.ds(off, tk), :], preferred_element_type=F32)
        acc_ref[u] = jnp.exp2(m_prev - m_new) * acc_ref[u] + pv
        m_ref[u] = m_new

    @pl.when(i == 0)
    def _():
        for u in range(2):
            scores(u, 0, 0)

    def body(jj, carry):
        for u in range(2):
            scores(u, 2 * jj + 1, 1)
        for u in range(2):
            accumulate(u, 2 * jj, 0, None)
        for u in range(2):
            scores(u, 2 * jj + 2, 0)
        for u in range(2):
            accumulate(u, 2 * jj + 1, 1, None)
        return carry

    lax.fori_loop(0, r * i, body, 0)
    base = 2 * r * i
    for d in range(2 * r):
        if d + 1 < 2 * r:
            for u in range(2):
                if d + 1 < r * (u + 1):
                    scores(u, base + d + 1, (d + 1) % 2)
        else:
            for u in range(2):
                scores(u, 0, 0, qn_ref)
        for u in range(2):
            if d < r * (u + 1):
                accumulate(u, base + d, d % 2, None if d < r * u else (r * u - d) * tk)
    for u in range(2):
        acc = acc_ref[u]
        o_ref[u * sub:(u + 1) * sub, :] = (acc[:, :MLA_V] / acc[:, MLA_V:]).astype(o_ref.dtype)


def _flash(q, k, v):
    nh, s, _ = q.shape
    sub = min(ATTN_SUB, s // 2)
    tk = min(ATTN_TK, sub)
    tq = 2 * sub
    n_blocks = s // tq
    q_spec = lambda index: pl.BlockSpec((1, tq, MLA_QK_PAD), index)
    return pl.pallas_call(
        functools.partial(_flash_kernel, sub=sub, tk=tk),
        grid=(nh, n_blocks),
        in_specs=[q_spec(lambda h, i: (h, i, 0)),
                  q_spec(lambda h, i: (h, jnp.minimum(i + 1, n_blocks - 1), 0)),
                  pl.BlockSpec((1, MLA_QK_PAD, s), lambda h, i: (h, 0, 0)),
                  pl.BlockSpec((1, s, 2 * MLA_V), lambda h, i: (h, 0, 0))],
        out_specs=pl.BlockSpec((tq, MLA_V), lambda h, i: (i, h)),
        out_shape=jax.ShapeDtypeStruct((s, nh * MLA_V), BF16),
        scratch_shapes=[pltpu.VMEM((2, 2, sub, tk), F32),
                        pltpu.VMEM((2, sub, 1), F32),
                        pltpu.VMEM((2, sub, 2 * MLA_V), F32)],
        compiler_params=_cparams(2),
        name="flash",
    )(q, q, k, v)


def _out_ffn_kernel(a_ref, x_ref, wo_ref, gf_ref, wgu_ref, wd_ref, gn_ref, xo_ref, *maybe_ho_ref):
    rows = x_ref.shape[0] // FFN_ROW_GROUPS
    parts = [slice(n * rows, (n + 1) * rows) for n in range(FFN_ROW_GROUPS)]
    x1 = [x_ref[r, :] + jnp.dot(a_ref[r, :], wo_ref[...], preferred_element_type=F32) for r in parts]
    h = [(_rms(v) * gf_ref[...]).astype(BF16) for v in x1]
    au = [jnp.dot(v, wgu_ref[...], preferred_element_type=F32) for v in h]
    p = [(_silu(v[:, :FFN_HIDDEN]) * v[:, FFN_HIDDEN:]).astype(BF16) for v in au]
    x2 = [v + jnp.dot(w, wd_ref[...], preferred_element_type=F32) for v, w in zip(x1, p)]
    for r, v in zip(parts, x2):
        xo_ref[r, :] = v
        for ho_ref in maybe_ho_ref:
            ho_ref[r, :] = (_rms(v) * gn_ref[...]).astype(ho_ref.dtype)


def _out_ffn(a, x, w_out, g_ffn, w_gate_up, w_down, layer, g_next, emit_next):
    s, d = x.shape
    tm = min(FFN_TM, s)
    row = lambda width: pl.BlockSpec((tm, width), lambda i: (i, 0))
    resident = lambda arr: pl.BlockSpec(arr.shape, lambda i: (0, 0), pipeline_mode=pl.Buffered(1))
    of_layer = lambda arr: pl.BlockSpec((None,) + arr.shape[1:], lambda i: (layer, 0, 0),
                                        pipeline_mode=pl.Buffered(1))
    gf = g_ffn.reshape(1, d)
    gn = g_next.reshape(1, d)
    n_out = 2 if emit_next else 1
    return pl.pallas_call(
        _out_ffn_kernel,
        grid=(s // tm,),
        in_specs=[row(a.shape[1]), row(d), resident(w_out), resident(gf), of_layer(w_gate_up),
                  of_layer(w_down), resident(gn)],
        out_specs=[row(d), row(d)][:n_out],
        out_shape=[jax.ShapeDtypeStruct((s, d), F32), jax.ShapeDtypeStruct((s, d), BF16)][:n_out],
        compiler_params=_cparams(1),
        name="out_ffn",
    )(a, x, w_out, gf, w_gate_up, w_down, gn)


def kernel(x, positions, norm_mix, norm_ffn, hgrn_w_in, hgrn_g_norm, hgrn_w_out, hgrn_lb_logits,
           gla_w_in, gla_w_gk_up, gla_b_gk, gla_g_norm, gla_w_out, ret_w_in, ret_w_out, mla_w_in,
           mla_g_q_lora, mla_g_kv_lora, mla_w_uq, mla_w_ukv, mla_g_qnorm, mla_g_knorm, mla_w_out,
           ffn_w_gate_up, ffn_w_down):
    b, s, d = x.shape
    depth = norm_mix.shape[0]
    lower_bounds = jnp.cumsum(jax.nn.softmax(hgrn_lb_logits.astype(F32), axis=0), axis=0)
    w_gate_up = ffn_w_gate_up.astype(BF16)
    w_down = ffn_w_down.astype(BF16)
    outs = []
    for bi in range(b):
        xs = x[bi]
        pos = positions[bi]
        cos, sin = _rope_angles(pos, RET_DK)
        step = RET_DK // MLA_ROPE
        h = _rmsnorm(xs, norm_mix[0])
        for i in range(depth):
            mixer, j = i % 4, i // 4
            if mixer == 0:
                w = hgrn_w_in[j]
                p = _proj(h, w, 3, lambda t: t + jnp.minimum(t, 1), BF16)
                pf = _proj(h, w, 1, lambda t: t + 1, F32)
                a = _hgrn_scan(p, pf, lower_bounds[i], hgrn_g_norm[j])
                w_out = hgrn_w_out[j]
            elif mixer == 1:
                main = 2 * GLA_KEY_DIM + 2 * GLA_VALUE_DIM
                pad = LANE - GLA_GATE_RANK
                p = _proj(h, gla_w_in[j], main // PROJ_TN, lambda t: t, BF16)
                w_low = jnp.pad(gla_w_in[j][:, main:], ((0, 0), (0, pad)))
                p_low = _proj(h, w_low, 1, lambda t: t, F32)
                w_up = jnp.pad(gla_w_gk_up[j], ((0, pad), (0, 0))).astype(BF16)
                a = _gla_scan(p, p_low, w_up, gla_b_gk[j], gla_g_norm[j])
                w_out = gla_w_out[j]
            elif mixer == 2:
                p = _proj(h, ret_w_in[j], ret_w_in.shape[2] // PROJ_TN, lambda t: t, BF16)
                a = _ret_scan(p, cos, sin)
                w_out = ret_w_out[j]
            else:
                q, k, v = _mla_prep(h, cos[:, ::step], sin[:, ::step], mla_w_in[j], mla_g_q_lora[j], mla_g_kv_lora[j],
                                    mla_w_uq[j], mla_w_ukv[j], mla_g_qnorm[j], mla_g_knorm[j])
                a = _flash(q, k, v)
                w_out = mla_w_out[j]
            last = i + 1 == depth
            res = _out_ffn(a, xs, w_out.astype(BF16), norm_ffn[i], w_gate_up, w_down, i,
                           norm_mix[0 if last else i + 1], not last)
            xs = res[0]
            h = None if last else res[1]
        outs.append(xs)
    return outs[0][None] if b == 1 else jnp.stack(outs, axis=0)
```

```python
import functools
import math
from typing import Any, NamedTuple

import numpy as np
import jax
import jax.numpy as jnp
from jax import lax
from jax.experimental import pallas as pl
from jax.experimental.pallas import tpu as pltpu

F32 = jnp.float32
BF16 = jnp.bfloat16

D_MODEL = 1024
NORM_EPS = 1e-6
ROPE_THETA = 10000.0
HGRN_HEADS, HGRN_DK, HGRN_DV = 8, 128, 128
GLA_HEADS, GLA_DK, GLA_DV = 4, 128, 256
GLA_KEY_DIM, GLA_VALUE_DIM, GLA_GATE_RANK = 512, 1024, 16
GLA_GATE_NORMALIZER = 16.0
RET_HEADS, RET_DK, RET_DV = 8, 128, 256
MLA_HEADS, MLA_Q_LORA, MLA_KV_LORA = 8, 384, 128
MLA_NOPE, MLA_ROPE, MLA_V = 128, 64, 128
MLA_QK = MLA_NOPE + MLA_ROPE
FFN_HIDDEN = 2816

LANE = 128
VMEM_LIMIT_BYTES = 56 * 1024 * 1024

SCAN_CHUNK = 128
SCAN_STEP_CHUNKS = 4
PROJ_TM, PROJ_TN = 2048, 1024
FFN_TM = 512
FFN_ROW_GROUPS = 2
MLA_TM = 256
ATTN_TK = 512
ATTN_SUB = 512

_NT = (((1,), (1,)), ((), ()))
_TN = (((0,), (0,)), ((), ()))


def _cparams(n_axes):
    return pltpu.CompilerParams(dimension_semantics=("arbitrary",) * n_axes,
                                vmem_limit_bytes=VMEM_LIMIT_BYTES)


def _rms(x, width=None):
    width = x.shape[-1] if width is None else width
    ss = jnp.sum(x * x, axis=-1, keepdims=True)
    return x * lax.rsqrt(ss * (1.0 / width) + NORM_EPS)


def _silu(x):
    return x * jax.nn.sigmoid(x)


def _rmsnorm_kernel(x_ref, g_ref, o_ref):
    o_ref[...] = (_rms(x_ref[...]) * g_ref[...]).astype(o_ref.dtype)


def _rmsnorm(x, gain, tm=1024):
    s, d = x.shape
    tm = min(tm, s)
    return pl.pallas_call(
        _rmsnorm_kernel,
        grid=(s // tm,),
        in_specs=[pl.BlockSpec((tm, d), lambda i: (i, 0)),
                  pl.BlockSpec((1, d), lambda i: (0, 0))],
        out_specs=pl.BlockSpec((tm, d), lambda i: (i, 0)),
        out_shape=jax.ShapeDtypeStruct((s, d), BF16),
        compiler_params=_cparams(1),
        name="rmsnorm",
    )(x, gain.reshape(1, d))


def _proj_kernel(h_ref, w_ref, o_ref):
    acc = jnp.dot(h_ref[...], w_ref[...].astype(BF16), preferred_element_type=F32)
    for c in range(acc.shape[1] // LANE):
        o_ref[c] = acc[:, c * LANE:(c + 1) * LANE].astype(o_ref.dtype)


def _proj(h, w, n_tiles, col_tile, out_dtype):
    s, k = h.shape
    tm = min(PROJ_TM, s)
    tn = min(PROJ_TN, w.shape[1])
    return pl.pallas_call(
        _proj_kernel,
        grid=(n_tiles, s // tm),
        in_specs=[pl.BlockSpec((tm, k), lambda j, i: (i, 0)),
                  pl.BlockSpec((k, tn), lambda j, i: (0, col_tile(j)))],
        out_specs=pl.BlockSpec((tn // LANE, tm, LANE), lambda j, i: (j, i, 0)),
        out_shape=jax.ShapeDtypeStruct((n_tiles * tn // LANE, s, LANE), out_dtype),
        compiler_params=_cparams(2),
        name="proj",
    )(h, w)


N_MATMUL_LEVELS = 3


def _scan_tables(chunk):
    n_levels = int(math.log2(chunk))
    t = np.arange(chunk)[:, None]
    r = np.arange(chunk)[None, :]
    blocks = [r <= t]
    level = np.full((chunk, chunk), -1, np.int32)
    level[np.arange(chunk), np.arange(chunk)] = n_levels
    for l in range(n_levels):
        m = 1 << l
        b = (t // (2 * m)) * (2 * m) + m - 1
        if 1 <= l < N_MATMUL_LEVELS:
            blocks.append(np.where(t > b, (r > b) & (r <= t), (r > t) & (r <= b)))
        same = (t // (2 * m)) == (r // (2 * m))
        level[same & (t % (2 * m) >= m) & (r % (2 * m) < m)] = l
    w = np.concatenate(blocks, axis=0).astype(np.float32)
    wcat = np.concatenate([w, w], axis=1)
    return jnp.asarray(wcat, BF16), jnp.asarray(level)


def _gate_sums(g, wcat_ref, cum_ref):
    c = g.shape[0]
    g_hi = g.astype(BF16)
    g_lo = (g - g_hi.astype(F32)).astype(BF16)
    gcat = jnp.concatenate([g_hi, g_lo], axis=0)
    sums = jnp.dot(wcat_ref[...], gcat, preferred_element_type=F32)
    cum_ref[...] = sums[:c]
    return sums[c:]


def _level_exponent(l, g, low, cum_ref, lanes):
    c = g.shape[0]
    if l == 0:
        odd = lax.broadcasted_iota(jnp.int32, g.shape, 0) % 2 == 1
        return jnp.where(odd, g, 0.0)
    if l < N_MATMUL_LEVELS:
        return low[(l - 1) * c:l * c]
    m = 1 << l
    parts = []
    for a in range(0, c, 2 * m):
        edge = cum_ref[a + m - 1:a + m, lanes]
        parts.append(edge - cum_ref[a:a + m, lanes])
        parts.append(cum_ref[a + m:a + 2 * m, lanes] - edge)
    return jnp.concatenate(parts, axis=0)


class _Head(NamedTuple):
    q: jax.Array
    k: jax.Array
    vb: jax.Array
    g: jax.Array
    low: jax.Array
    lanes: slice
    st_ref: Any


def _gated_intra(h, cum_ref, lv):
    c = h.q.shape[0]
    n_levels = int(math.log2(c))
    scores = jnp.where(lv == n_levels,
                       lax.dot_general(h.q.astype(BF16), h.k.astype(BF16), _NT,
                                       preferred_element_type=F32), 0.0)
    for l in range(n_levels):
        e = jnp.exp2(_level_exponent(l, h.g, h.low, cum_ref, h.lanes))
        s = jnp.dot((h.q * e).astype(BF16), (h.k * e).T.astype(BF16), preferred_element_type=F32)
        scores = jnp.where(lv == l, s, scores)
    return jnp.dot(scores.astype(BF16), h.vb, preferred_element_type=F32)


def _gated_carry(h, o_intra, cum_ref):
    c = h.q.shape[0]
    cum = cum_ref[:, h.lanes]
    last = cum_ref[c - 1:c, h.lanes]
    q_in = (h.q * jnp.exp2(cum)).astype(BF16)
    k_out = (h.k * jnp.exp2(last - cum)).astype(BF16)
    st = h.st_ref[...]
    o = o_intra + lax.dot_general(q_in, st.astype(BF16), _NT, preferred_element_type=F32)
    h.st_ref[...] = st * jnp.exp2(last) + lax.dot_general(h.vb, k_out, _TN, preferred_element_type=F32)
    return o


def _reset_state(st_ref):
    @pl.when(pl.program_id(0) == 0)
    def _():
        st_ref[...] = jnp.zeros_like(st_ref)


def _cols(p_ref, first, n, rows):
    if n == 1:
        return p_ref[first, rows, :]
    return jnp.concatenate([p_ref[first + i, rows, :] for i in range(n)], axis=1)


def _chunk_rows(ref):
    return [slice(r, r + SCAN_CHUNK) for r in range(0, ref.shape[1], SCAN_CHUNK)]


def _hgrn_kernel(p_ref, f_ref, lb_ref, gn_ref, wcat_ref, lv_ref, o_ref, st_ref, cum_ref):
    _reset_state(st_ref)
    nh = HGRN_HEADS
    lb = lb_ref[...]
    lv = lv_ref[...]
    work = []
    for ci, rows in enumerate(_chunk_rows(p_ref)):
        forget = lb + (1.0 - lb) * jax.nn.sigmoid(_cols(f_ref, 0, nh, rows))
        g = jnp.log2(forget)
        low = _gate_sums(g, wcat_ref, cum_ref.at[ci])
        for hd in range(nh):
            lanes = slice(hd * LANE, (hd + 1) * LANE)
            q = _silu(p_ref[hd, rows, :].astype(F32)) * (HGRN_DK ** -0.5)
            h = _Head(q, 1.0 - forget[:, lanes], p_ref[nh + hd, rows, :], g[:, lanes], low[:, lanes],
                      lanes, st_ref.at[hd])
            work.append((ci, rows, hd, h, _gated_intra(h, cum_ref.at[ci], lv)))
    for ci, rows, hd, h, o_intra in work:
        o = _gated_carry(h, o_intra, cum_ref.at[ci])
        o = _rms(o) * gn_ref[...] * _silu(p_ref[2 * nh + hd, rows, :].astype(F32))
        o_ref[rows, hd * LANE:(hd + 1) * LANE] = o.astype(o_ref.dtype)


def _scan_call(body, p, pg, consts, n_heads, dk, dv, name):
    s = p.shape[1]
    c = min(SCAN_CHUNK, s)
    n_chunks = min(SCAN_STEP_CHUNKS, s // c)
    tm = c * n_chunks
    const = lambda a: pl.BlockSpec(a.shape, lambda t: (0,) * a.ndim)
    blocks = lambda a: pl.BlockSpec((a.shape[0], tm, LANE), lambda t: (0, t, 0))
    return pl.pallas_call(
        body,
        grid=(s // tm,),
        in_specs=[blocks(p), blocks(pg)] + [const(a) for a in consts],
        out_specs=pl.BlockSpec((tm, n_heads * dv), lambda t: (t, 0)),
        out_shape=jax.ShapeDtypeStruct((s, n_heads * dv), BF16),
        scratch_shapes=[pltpu.VMEM((n_heads, dv, dk), F32),
                        pltpu.VMEM((n_chunks, c, n_heads * dk), F32)],
        compiler_params=_cparams(1),
        name=name,
    )(p, pg, *consts)


def _hgrn_scan(p, pf, lower_bound, g_norm):
    wcat, level = _scan_tables(min(SCAN_CHUNK, p.shape[1]))
    consts = (lower_bound.reshape(1, -1), g_norm.reshape(1, HGRN_DV), wcat, level)
    return _scan_call(_hgrn_kernel, p, pf, consts, HGRN_HEADS, HGRN_DK, HGRN_DV, "hgrn_scan")


def _log_sigmoid(z):
    return jnp.minimum(z, 0.0) - jnp.log(1.0 + jnp.exp(-jnp.abs(z)))


def _gla_kernel(p_ref, low_ref, wup_ref, b_ref, gn_ref, wcat_ref, lv_ref, o_ref, st_ref, cum_ref):
    _reset_state(st_ref)
    nh = GLA_HEADS
    nv = GLA_DV // LANE
    lv = lv_ref[...]
    work = []
    for ci, rows in enumerate(_chunk_rows(p_ref)):
        z = jnp.dot(low_ref[0, rows, :].astype(BF16), wup_ref[...], preferred_element_type=F32) + b_ref[...]
        g = _log_sigmoid(z) * (math.log2(math.e) / GLA_GATE_NORMALIZER)
        low = _gate_sums(g, wcat_ref, cum_ref.at[ci])
        for hd in range(nh):
            lanes = slice(hd * LANE, (hd + 1) * LANE)
            q = p_ref[hd, rows, :].astype(F32) * (GLA_DK ** -0.5)
            h = _Head(q, p_ref[nh + hd, rows, :].astype(F32), _cols(p_ref, 2 * nh + nv * hd, nv, rows),
                      g[:, lanes], low[:, lanes], lanes, st_ref.at[hd])
            work.append((ci, rows, hd, h, _gated_intra(h, cum_ref.at[ci], lv)))
    for ci, rows, hd, h, o_intra in work:
        o = _gated_carry(h, o_intra, cum_ref.at[ci])
        gate = _cols(p_ref, 2 * nh + nv * nh + nv * hd, nv, rows).astype(F32)
        o = _rms(o) * gn_ref[...] * _silu(gate)
        o_ref[rows, hd * GLA_DV:(hd + 1) * GLA_DV] = o.astype(o_ref.dtype)


def _gla_scan(p, p_low, w_up, b_gk, g_norm):
    wcat, level = _scan_tables(min(SCAN_CHUNK, p.shape[1]))
    consts = (w_up, b_gk.reshape(1, -1), g_norm.reshape(1, GLA_DV), wcat, level)
    return _scan_call(_gla_kernel, p, p_low, consts, GLA_HEADS, GLA_DK, GLA_DV, "gla_scan")


def _ret_kernel(p_ref, cos_ref, sin_ref, dm_ref, qd_ref, kd_ref, cd_ref, o_ref, st_ref):
    _reset_state(st_ref)
    nh = RET_HEADS
    nv = RET_DV // LANE
    half = RET_DK // 2
    work = []
    for rows in _chunk_rows(p_ref):
        cos = cos_ref[rows, :]
        sin = sin_ref[rows, :]
        qkv = []
        for hd in range(nh):
            q = p_ref[hd, rows, :].astype(F32)
            k = p_ref[nh + hd, rows, :].astype(F32)
            q = q * cos + pltpu.roll(q, half, 1) * sin
            k = (k * cos + pltpu.roll(k, half, 1) * sin) * (RET_DK ** -0.5)
            qkv.append((q, k, _cols(p_ref, 2 * nh + nv * hd, nv, rows)))
        for hd, (q, k, vb) in enumerate(qkv):
            scores = lax.dot_general(q.astype(BF16), k.astype(BF16), _NT,
                                     preferred_element_type=F32) * dm_ref[hd]
            work.append((rows, hd, q, k, vb, jnp.dot(scores.astype(BF16), vb, preferred_element_type=F32)))
    for n in range(0, len(work), nh):
        outs = []
        for rows, hd, q, k, vb, o in work[n:n + nh]:
            st = st_ref[hd]
            o = o + lax.dot_general((q * qd_ref[hd]).astype(BF16), st.astype(BF16), _NT,
                                    preferred_element_type=F32)
            st_ref[hd] = st * cd_ref[hd] + lax.dot_general(vb, (k * kd_ref[hd]).astype(BF16), _TN,
                                                           preferred_element_type=F32)
            outs.append(o)
        for (rows, hd, *_), o in zip(work[n:n + nh], outs):
            o = _rms(o) * _silu(_cols(p_ref, 2 * nh + nv * nh + nv * hd, nv, rows).astype(F32))
            o_ref[rows, hd * RET_DV:(hd + 1) * RET_DV] = o.astype(o_ref.dtype)


def _rope_angles(positions, dim):
    half = dim // 2
    inv_freq = 1.0 / (ROPE_THETA ** (jnp.arange(half, dtype=F32) / half))
    ang = positions.astype(F32)[:, None] * inv_freq
    return jnp.cos(ang), jnp.sin(ang)


def _rope_tables(cos, sin, pad_to):
    pad = jnp.zeros((cos.shape[0], pad_to // 2 - cos.shape[1]), F32)
    cos_t = jnp.concatenate([cos, pad, cos, pad], axis=1)
    sin_t = jnp.concatenate([-sin, pad, sin, pad], axis=1)
    return cos_t, sin_t


def _ret_scan(p, cos, sin):
    nb, s, _ = p.shape
    c = min(SCAN_CHUNK, s)
    h = RET_HEADS
    cos_t, sin_t = _rope_tables(cos, sin, LANE)
    log_gamma = jnp.log(1.0 - 2.0 ** (-5.0 - jnp.arange(h, dtype=F32)))
    idx = jnp.arange(c, dtype=F32)
    causal = jnp.tril(jnp.ones((c, c), dtype=bool))
    lg = log_gamma[:, None, None]
    dm = jnp.exp(jnp.where(causal, lg * (idx[:, None] - idx[None, :]), -jnp.inf))
    ones = jnp.ones((1, 1, LANE), F32)
    qd = jnp.exp(lg * (idx[None, :, None] + 1.0)) * ones
    kd = jnp.exp(lg * (c - 1.0 - idx[None, :, None])) * ones
    cd = jnp.exp(lg * float(c)) * ones
    tm = c * min(SCAN_STEP_CHUNKS, s // c)
    tab = pl.BlockSpec((tm, LANE), lambda t: (t, 0))
    const = lambda a: pl.BlockSpec(a.shape, lambda t: (0,) * a.ndim)
    return pl.pallas_call(
        _ret_kernel,
        grid=(s // tm,),
        in_specs=[pl.BlockSpec((nb, tm, LANE), lambda t: (0, t, 0)), tab, tab,
                  const(dm), const(qd), const(kd), const(cd)],
        out_specs=pl.BlockSpec((tm, h * RET_DV), lambda t: (t, 0)),
        out_shape=jax.ShapeDtypeStruct((s, h * RET_DV), BF16),
        scratch_shapes=[pltpu.VMEM((h, RET_DV, RET_DK), F32)],
        compiler_params=_cparams(1),
        name="ret_scan",
    )(p, cos_t, sin_t, dm, qd, kd, cd)


MLA_QK_PAD = 2 * LANE


def _mla_prep_kernel(h_ref, win_ref, gq_ref, gkv_ref, wuq_ref, wukt_ref, wuv_ref, gqn_ref, gkn_ref,
                     cos_ref, sin_ref, cos_t_ref, sin_t_ref, q_ref, k_ref, v_ref):
    nh = MLA_HEADS
    c = jnp.dot(h_ref[...], win_ref[...], preferred_element_type=F32)
    c_q = _rms(c[:, :MLA_Q_LORA]) * gq_ref[...]
    c_kv = (_rms(c[:, MLA_Q_LORA:MLA_Q_LORA + MLA_KV_LORA]) * gkv_ref[...]).astype(BF16)
    qf = jnp.dot(c_q.astype(BF16), wuq_ref[...], preferred_element_type=F32)
    vf = jnp.dot(c_kv, wuv_ref[...], preferred_element_type=F32)
    knt = lax.dot_general(wukt_ref[...], c_kv, _NT, preferred_element_type=F32)
    krt = c[:, MLA_Q_LORA + MLA_KV_LORA:].T
    kr_ss = jnp.sum(krt * krt, axis=0, keepdims=True)
    cos = cos_ref[...]
    sin = sin_ref[...]
    cos_t = cos_t_ref[...]
    sin_t = sin_t_ref[...]
    gqn = gqn_ref[...]
    gkn = gkn_ref[...]
    half = LANE // 2
    scale = MLA_QK ** -0.5 * math.log2(math.e)
    for hd in range(nh):
        q_n = qf[:, hd * LANE:(hd + 1) * LANE]
        q_r = qf[:, (nh + hd) * LANE:(nh + hd + 1) * LANE]
        ss = jnp.sum(q_n * q_n + q_r * q_r, axis=-1, keepdims=True)
        r = lax.rsqrt(ss * (1.0 / MLA_QK) + NORM_EPS)
        q_n = q_n * r * gqn[:, :LANE]
        q_r = q_r * r * gqn[:, LANE:]
        q_r = q_r * cos + pltpu.roll(q_r, half, 1) * sin
        q_ref[hd] = (jnp.concatenate([q_n, q_r], axis=1) * scale).astype(q_ref.dtype)
        k_n = knt[hd * LANE:(hd + 1) * LANE]
        ss = jnp.sum(k_n * k_n, axis=0, keepdims=True) + kr_ss
        r = lax.rsqrt(ss * (1.0 / MLA_QK) + NORM_EPS)
        k_n = k_n * r * gkn[:LANE]
        k_r = krt * r * gkn[LANE:]
        k_r = k_r * cos_t + jnp.concatenate([k_r[half:], k_r[:half]], axis=0) * sin_t
        k_ref[hd] = jnp.concatenate([k_n, k_r], axis=0).astype(k_ref.dtype)
        v_h = vf[:, hd * LANE:(hd + 1) * LANE]
        v_ref[hd] = jnp.concatenate([v_h, jnp.ones_like(v_h)], axis=1).astype(v_ref.dtype)


def _pad_rope_cols(w):
    half = MLA_ROPE // 2
    z = jnp.zeros(w.shape[:-1] + (LANE // 2 - half,), w.dtype)
    return jnp.concatenate([w[..., :half], z, w[..., half:], z], axis=-1)


def _mla_prep(h, cos, sin, w_in, g_q_lora, g_kv_lora, w_uq, w_ukv, g_qnorm, g_knorm):
    s = h.shape[0]
    tm = min(MLA_TM, s)
    nh = MLA_HEADS
    lat = MLA_Q_LORA + MLA_KV_LORA
    w_in_p = jnp.concatenate([w_in[:, :lat], _pad_rope_cols(w_in[:, lat:])], axis=1).astype(BF16)
    wq = w_uq.reshape(MLA_Q_LORA, nh, MLA_QK)
    wq_nope = wq[:, :, :MLA_NOPE].reshape(MLA_Q_LORA, nh * LANE)
    wq_rope = _pad_rope_cols(wq[:, :, MLA_NOPE:]).reshape(MLA_Q_LORA, nh * LANE)
    w_uq_p = jnp.concatenate([wq_nope, wq_rope], axis=1).astype(BF16)
    wkv = w_ukv.reshape(MLA_KV_LORA, nh, MLA_NOPE + MLA_V)
    w_uk_t = wkv[:, :, :MLA_NOPE].reshape(MLA_KV_LORA, nh * LANE).T.astype(BF16)
    w_uv = wkv[:, :, MLA_NOPE:].reshape(MLA_KV_LORA, nh * LANE).astype(BF16)
    pad_gain = lambda g: jnp.concatenate([g[:MLA_NOPE], _pad_rope_cols(g[MLA_NOPE:])])
    cos_t, sin_t = _rope_tables(cos, sin, LANE)
    const = lambda a: pl.BlockSpec(a.shape, lambda i: (0,) * a.ndim)
    args = (w_in_p, g_q_lora.reshape(1, -1), g_kv_lora.reshape(1, -1), w_uq_p, w_uk_t, w_uv,
            pad_gain(g_qnorm).reshape(1, -1), pad_gain(g_knorm).reshape(-1, 1))
    tab = pl.BlockSpec((tm, LANE), lambda i: (i, 0))
    tab_t = pl.BlockSpec((LANE, tm), lambda i: (0, i))
    return pl.pallas_call(
        _mla_prep_kernel,
        grid=(s // tm,),
        in_specs=([pl.BlockSpec((tm, D_MODEL), lambda i: (i, 0))] + [const(a) for a in args]
                  + [tab, tab, tab_t, tab_t]),
        out_specs=[pl.BlockSpec((nh, tm, MLA_QK_PAD), lambda i: (0, i, 0)),
                   pl.BlockSpec((nh, MLA_QK_PAD, tm), lambda i: (0, 0, i)),
                   pl.BlockSpec((nh, tm, 2 * MLA_V), lambda i: (0, i, 0))],
        out_shape=[jax.ShapeDtypeStruct((nh, s, MLA_QK_PAD), BF16),
                   jax.ShapeDtypeStruct((nh, MLA_QK_PAD, s), BF16),
                   jax.ShapeDtypeStruct((nh, s, 2 * MLA_V), BF16)],
        compiler_params=_cparams(1),
        name="mla_prep",
    )(h, *args, cos_t, sin_t, cos_t.T, sin_t.T)


def _flash_kernel(q_ref, qn_ref, k_ref, v_ref, o_ref, s_ref, mx_ref, m_ref, acc_ref, *, sub, tk):
    r = sub // tk
    i = pl.program_id(1)
    m_ref[...] = jnp.full(m_ref.shape, -jnp.inf, F32)
    acc_ref[...] = jnp.zeros(acc_ref.shape, F32)

    def scores(u, j, dst, src_ref=q_ref):
        off = pl.multiple_of(j * tk, tk)
        sc = jnp.dot(src_ref[0, u * sub:(u + 1) * sub, :], k_ref[0, :, pl.ds(off, tk)],
                     preferred_element_type=F32)
        s_ref[dst, u] = sc
        mx_ref[dst, u] = jnp.broadcast_to(jnp.max(sc, axis=-1, keepdims=True), (sub, LANE))

    def accumulate(u, j, src, shift):
        off = pl.multiple_of(j * tk, tk)
        s = s_ref[src, u]
        if shift is not None:
            row = lax.broadcasted_iota(jnp.int32, (sub, tk), 0)
            col = lax.broadcasted_iota(jnp.int32, (sub, tk), 1)
            s = jnp.where(col - row <= shift, s, -jnp.inf)
        m_prev = m_ref[u]
        m_new = jnp.maximum(m_prev, mx_ref[src, u] if shift is None else jnp.max(s, axis=-1, keepdims=True))
        p = jnp.exp2(s - jnp.concatenate([m_new] * (tk // LANE), axis=1)).astype(BF16)
        pv = jnp.dot(p, v_ref[0, pl.ds(off, tk), :], preferred_element_type=F32)
        alpha = jnp.exp2(m_prev - m_new)
        acc_ref[u] = jnp.concatenate([alpha] * (2 * MLA_V // LANE), axis=1) * acc_ref[u] + pv
        m_ref[u] = m_new

    @pl.when(i == 0)
    def _():
        for u in range(2):
            scores(u, 0, 0)

    def body(jj, carry):
        for u in range(2):
            scores(u, 2 * jj + 1, 1)
        for u in range(2):
            accumulate(u, 2 * jj, 0, None)
        for u in range(2):
            scores(u, 2 * jj + 2, 0)
        for u in range(2):
            accumulate(u, 2 * jj + 1, 1, None)
        return carry

    lax.fori_loop(0, r * i, body, 0)
    base = 2 * r * i
    for d in range(2 * r):
        if d + 1 < 2 * r:
            for u in range(2):
                if d + 1 < r * (u + 1):
                    scores(u, base + d + 1, (d + 1) % 2)
        else:
            for u in range(2):
                scores(u, 0, 0, qn_ref)
        for u in range(2):
            if d < r * (u + 1):
                accumulate(u, base + d, d % 2, None if d < r * u else (r * u - d) * tk)
    for u in range(2):
        acc = acc_ref[u]
        o_ref[u * sub:(u + 1) * sub, :] = (acc[:, :MLA_V] / acc[:, MLA_V:]).astype(o_ref.dtype)


def _flash(q, k, v):
    nh, s, _ = q.shape
    sub = min(ATTN_SUB, s // 2)
    tk = min(ATTN_TK, sub)
    tq = 2 * sub
    n_blocks = s // tq
    q_spec = lambda index: pl.BlockSpec((1, tq, MLA_QK_PAD), index)
    return pl.pallas_call(
        functools.partial(_flash_kernel, sub=sub, tk=tk),
        grid=(nh, n_blocks),
        in_specs=[q_spec(lambda h, i: (h, i, 0)),
                  q_spec(lambda h, i: (h, jnp.minimum(i + 1, n_blocks - 1), 0)),
                  pl.BlockSpec((1, MLA_QK_PAD, s), lambda h, i: (h, 0, 0)),
                  pl.BlockSpec((1, s, 2 * MLA_V), lambda h, i: (h, 0, 0))],
        out_specs=pl.BlockSpec((tq, MLA_V), lambda h, i: (i, h)),
        out_shape=jax.ShapeDtypeStruct((s, nh * MLA_V), BF16),
        scratch_shapes=[pltpu.VMEM((2, 2, sub, tk), F32),
                        pltpu.VMEM((2, 2, sub, LANE), F32),
                        pltpu.VMEM((2, sub, LANE), F32),
                        pltpu.VMEM((2, sub, 2 * MLA_V), F32)],
        compiler_params=_cparams(2),
        name="flash",
    )(q, q, k, v)


def _out_ffn_kernel(a_ref, x_ref, wo_ref, gf_ref, wgu_ref, wd_ref, gn_ref, xo_ref, *maybe_ho_ref):
    rows = x_ref.shape[0] // FFN_ROW_GROUPS
    parts = [slice(n * rows, (n + 1) * rows) for n in range(FFN_ROW_GROUPS)]
    x1 = [x_ref[r, :] + jnp.dot(a_ref[r, :], wo_ref[...], preferred_element_type=F32) for r in parts]
    h = [(_rms(v) * gf_ref[...]).astype(BF16) for v in x1]
    au = [jnp.dot(v, wgu_ref[...], preferred_element_type=F32) for v in h]
    p = [(_silu(v[:, :FFN_HIDDEN]) * v[:, FFN_HIDDEN:]).astype(BF16) for v in au]
    x2 = [v + jnp.dot(w, wd_ref[...], preferred_element_type=F32) for v, w in zip(x1, p)]
    for r, v in zip(parts, x2):
        xo_ref[r, :] = v
        for ho_ref in maybe_ho_ref:
            ho_ref[r, :] = (_rms(v) * gn_ref[...]).astype(ho_ref.dtype)


def _out_ffn(a, x, w_out, g_ffn, w_gate_up, w_down, layer, g_next, emit_next):
    s, d = x.shape
    tm = min(FFN_TM, s)
    row = lambda width: pl.BlockSpec((tm, width), lambda i: (i, 0))
    resident = lambda arr: pl.BlockSpec(arr.shape, lambda i: (0, 0), pipeline_mode=pl.Buffered(1))
    of_layer = lambda arr: pl.BlockSpec((None,) + arr.shape[1:], lambda i: (layer, 0, 0),
                                        pipeline_mode=pl.Buffered(1))
    gf = g_ffn.reshape(1, d)
    gn = g_next.reshape(1, d)
    n_out = 2 if emit_next else 1
    return pl.pallas_call(
        _out_ffn_kernel,
        grid=(s // tm,),
        in_specs=[row(a.shape[1]), row(d), resident(w_out), resident(gf), of_layer(w_gate_up),
                  of_layer(w_down), resident(gn)],
        out_specs=[row(d), row(d)][:n_out],
        out_shape=[jax.ShapeDtypeStruct((s, d), F32), jax.ShapeDtypeStruct((s, d), BF16)][:n_out],
        compiler_params=_cparams(1),
        name="out_ffn",
    )(a, x, w_out, gf, w_gate_up, w_down, gn)


def kernel(x, positions, norm_mix, norm_ffn, hgrn_w_in, hgrn_g_norm, hgrn_w_out, hgrn_lb_logits,
           gla_w_in, gla_w_gk_up, gla_b_gk, gla_g_norm, gla_w_out, ret_w_in, ret_w_out, mla_w_in,
           mla_g_q_lora, mla_g_kv_lora, mla_w_uq, mla_w_ukv, mla_g_qnorm, mla_g_knorm, mla_w_out,
           ffn_w_gate_up, ffn_w_down):
    b, s, d = x.shape
    depth = norm_mix.shape[0]
    lower_bounds = jnp.cumsum(jax.nn.softmax(hgrn_lb_logits.astype(F32), axis=0), axis=0)
    w_gate_up = ffn_w_gate_up.astype(BF16)
    w_down = ffn_w_down.astype(BF16)
    outs = []
    for bi in range(b):
        xs = x[bi]
        pos = positions[bi]
        cos, sin = _rope_angles(pos, RET_DK)
        step = RET_DK // MLA_ROPE
        h = _rmsnorm(xs, norm_mix[0])
        for i in range(depth):
            mixer, j = i % 4, i // 4
            if mixer == 0:
                w = hgrn_w_in[j]
                p = _proj(h, w, 3, lambda t: t + jnp.minimum(t, 1), BF16)
                pf = _proj(h, w, 1, lambda t: t + 1, F32)
                a = _hgrn_scan(p, pf, lower_bounds[i], hgrn_g_norm[j])
                w_out = hgrn_w_out[j]
            elif mixer == 1:
                main = 2 * GLA_KEY_DIM + 2 * GLA_VALUE_DIM
                pad = LANE - GLA_GATE_RANK
                p = _proj(h, gla_w_in[j], main // PROJ_TN, lambda t: t, BF16)
                w_low = jnp.pad(gla_w_in[j][:, main:], ((0, 0), (0, pad)))
                p_low = _proj(h, w_low, 1, lambda t: t, F32)
                w_up = jnp.pad(gla_w_gk_up[j], ((0, pad), (0, 0))).astype(BF16)
                a = _gla_scan(p, p_low, w_up, gla_b_gk[j], gla_g_norm[j])
                w_out = gla_w_out[j]
            elif mixer == 2:
                p = _proj(h, ret_w_in[j], ret_w_in.shape[2] // PROJ_TN, lambda t: t, BF16)
                a = _ret_scan(p, cos, sin)
                w_out = ret_w_out[j]
            else:
                q, k, v = _mla_prep(h, cos[:, ::step], sin[:, ::step], mla_w_in[j], mla_g_q_lora[j], mla_g_kv_lora[j],
                                    mla_w_uq[j], mla_w_ukv[j], mla_g_qnorm[j], mla_g_knorm[j])
                a = _flash(q, k, v)
                w_out = mla_w_out[j]
            last = i + 1 == depth
            res = _out_ffn(a, xs, w_out.astype(BF16), norm_ffn[i], w_gate_up, w_down, i,
                           norm_mix[0 if last else i + 1], not last)
            xs = res[0]
            h = None if last else res[1]
        outs.append(xs)
    return outs[0][None] if b == 1 else jnp.stack(outs, axis=0)
```

```python
import functools
import math
from typing import Any, NamedTuple

import numpy as np
import jax
import jax.numpy as jnp
from jax import lax
from jax.experimental import pallas as pl
from jax.experimental.pallas import tpu as pltpu

F32 = jnp.float32
BF16 = jnp.bfloat16

D_MODEL = 1024
NORM_EPS = 1e-6
ROPE_THETA = 10000.0
HGRN_HEADS, HGRN_DK, HGRN_DV = 8, 128, 128
GLA_HEADS, GLA_DK, GLA_DV = 4, 128, 256
GLA_KEY_DIM, GLA_VALUE_DIM, GLA_GATE_RANK = 512, 1024, 16
GLA_GATE_NORMALIZER = 16.0
RET_HEADS, RET_DK, RET_DV = 8, 128, 256
MLA_HEADS, MLA_Q_LORA, MLA_KV_LORA = 8, 384, 128
MLA_NOPE, MLA_ROPE, MLA_V = 128, 64, 128
MLA_QK = MLA_NOPE + MLA_ROPE
FFN_HIDDEN = 2816

LANE = 128
VMEM_LIMIT_BYTES = 56 * 1024 * 1024

SCAN_CHUNK = 128
SCAN_STEP_CHUNKS = 4
PROJ_TM, PROJ_TN = 2048, 1024
FFN_TM = 512
FFN_ROW_GROUPS = 2
MLA_TM = 256
ATTN_TK = 512
ATTN_SUB = 512

_NT = (((1,), (1,)), ((), ()))
_TN = (((0,), (0,)), ((), ()))


def _cparams(n_axes):
    return pltpu.CompilerParams(dimension_semantics=("arbitrary",) * n_axes,
                                vmem_limit_bytes=VMEM_LIMIT_BYTES)


def _rms(x, width=None):
    width = x.shape[-1] if width is None else width
    ss = jnp.sum(x * x, axis=-1, keepdims=True)
    return x * lax.rsqrt(ss * (1.0 / width) + NORM_EPS)


def _silu(x):
    return x * jax.nn.sigmoid(x)


def _rmsnorm_kernel(x_ref, g_ref, o_ref):
    o_ref[...] = (_rms(x_ref[...]) * g_ref[...]).astype(o_ref.dtype)


def _rmsnorm(x, gain, tm=1024):
    s, d = x.shape
    tm = min(tm, s)
    return pl.pallas_call(
        _rmsnorm_kernel,
        grid=(s // tm,),
        in_specs=[pl.BlockSpec((tm, d), lambda i: (i, 0)),
                  pl.BlockSpec((1, d), lambda i: (0, 0))],
        out_specs=pl.BlockSpec((tm, d), lambda i: (i, 0)),
        out_shape=jax.ShapeDtypeStruct((s, d), BF16),
        compiler_params=_cparams(1),
        name="rmsnorm",
    )(x, gain.reshape(1, d))


def _proj_kernel(h_ref, w_ref, o_ref):
    acc = jnp.dot(h_ref[...], w_ref[...].astype(BF16), preferred_element_type=F32)
    for c in range(acc.shape[1] // LANE):
        o_ref[c] = acc[:, c * LANE:(c + 1) * LANE].astype(o_ref.dtype)


def _proj(h, w, n_tiles, col_tile, out_dtype):
    s, k = h.shape
    tm = min(PROJ_TM, s)
    tn = min(PROJ_TN, w.shape[1])
    return pl.pallas_call(
        _proj_kernel,
        grid=(n_tiles, s // tm),
        in_specs=[pl.BlockSpec((tm, k), lambda j, i: (i, 0)),
                  pl.BlockSpec((k, tn), lambda j, i: (0, col_tile(j)))],
        out_specs=pl.BlockSpec((tn // LANE, tm, LANE), lambda j, i: (j, i, 0)),
        out_shape=jax.ShapeDtypeStruct((n_tiles * tn // LANE, s, LANE), out_dtype),
        compiler_params=_cparams(2),
        name="proj",
    )(h, w)


N_MATMUL_LEVELS = 3


def _scan_tables(chunk):
    n_levels = int(math.log2(chunk))
    t = np.arange(chunk)[:, None]
    r = np.arange(chunk)[None, :]
    blocks = [r <= t]
    level = np.full((chunk, chunk), -1, np.int32)
    level[np.arange(chunk), np.arange(chunk)] = n_levels
    for l in range(n_levels):
        m = 1 << l
        b = (t // (2 * m)) * (2 * m) + m - 1
        if 1 <= l < N_MATMUL_LEVELS:
            blocks.append(np.where(t > b, (r > b) & (r <= t), (r > t) & (r <= b)))
        same = (t // (2 * m)) == (r // (2 * m))
        level[same & (t % (2 * m) >= m) & (r % (2 * m) < m)] = l
    w = np.concatenate(blocks, axis=0).astype(np.float32)
    wcat = np.concatenate([w, w], axis=1)
    return jnp.asarray(wcat, BF16), jnp.asarray(level)


def _gate_sums(g, wcat_ref, cum_ref):
    c = g.shape[0]
    g_hi = g.astype(BF16)
    g_lo = (g - g_hi.astype(F32)).astype(BF16)
    gcat = jnp.concatenate([g_hi, g_lo], axis=0)
    sums = jnp.dot(wcat_ref[...], gcat, preferred_element_type=F32)
    cum_ref[...] = sums[:c]
    return sums[c:]


def _level_exponent(l, g, low, cum_ref, lanes):
    c = g.shape[0]
    if l == 0:
        odd = lax.broadcasted_iota(jnp.int32, g.shape, 0) % 2 == 1
        return jnp.where(odd, g, 0.0)
    if l < N_MATMUL_LEVELS:
        return low[(l - 1) * c:l * c]
    m = 1 << l
    parts = []
    for a in range(0, c, 2 * m):
        edge = cum_ref[a + m - 1:a + m, lanes]
        parts.append(edge - cum_ref[a:a + m, lanes])
        parts.append(cum_ref[a + m:a + 2 * m, lanes] - edge)
    return jnp.concatenate(parts, axis=0)


class _Head(NamedTuple):
    q: jax.Array
    k: jax.Array
    vb: jax.Array
    g: jax.Array
    low: jax.Array
    lanes: slice
    st_ref: Any


def _gated_intra(h, cum_ref, lv):
    c = h.q.shape[0]
    n_levels = int(math.log2(c))
    scores = jnp.where(lv == n_levels,
                       lax.dot_general(h.q.astype(BF16), h.k.astype(BF16), _NT,
                                       preferred_element_type=F32), 0.0)
    for l in range(n_levels):
        e = jnp.exp2(_level_exponent(l, h.g, h.low, cum_ref, h.lanes))
        s = jnp.dot((h.q * e).astype(BF16), (h.k * e).T.astype(BF16), preferred_element_type=F32)
        scores = jnp.where(lv == l, s, scores)
    return jnp.dot(scores.astype(BF16), h.vb, preferred_element_type=F32)


def _gated_carry(h, o_intra, cum_ref):
    c = h.q.shape[0]
    cum = cum_ref[:, h.lanes]
    last = cum_ref[c - 1:c, h.lanes]
    q_in = (h.q * jnp.exp2(cum)).astype(BF16)
    k_out = (h.k * jnp.exp2(last - cum)).astype(BF16)
    st = h.st_ref[...]
    o = o_intra + lax.dot_general(q_in, st.astype(BF16), _NT, preferred_element_type=F32)
    h.st_ref[...] = st * jnp.exp2(last) + lax.dot_general(h.vb, k_out, _TN, preferred_element_type=F32)
    return o


def _reset_state(st_ref):
    @pl.when(pl.program_id(0) == 0)
    def _():
        st_ref[...] = jnp.zeros_like(st_ref)


def _cols(p_ref, first, n, rows):
    if n == 1:
        return p_ref[first, rows, :]
    return jnp.concatenate([p_ref[first + i, rows, :] for i in range(n)], axis=1)


def _chunk_rows(ref):
    return [slice(r, r + SCAN_CHUNK) for r in range(0, ref.shape[1], SCAN_CHUNK)]


def _hgrn_kernel(p_ref, f_ref, lb_ref, gn_ref, wcat_ref, lv_ref, o_ref, st_ref, cum_ref):
    _reset_state(st_ref)
    nh = HGRN_HEADS
    lb = lb_ref[...]
    lv = lv_ref[...]
    work = []
    for ci, rows in enumerate(_chunk_rows(p_ref)):
        forget = lb + (1.0 - lb) * jax.nn.sigmoid(_cols(f_ref, 0, nh, rows))
        g = jnp.log2(forget)
        low = _gate_sums(g, wcat_ref, cum_ref.at[ci])
        for hd in range(nh):
            lanes = slice(hd * LANE, (hd + 1) * LANE)
            q = _silu(p_ref[hd, rows, :].astype(F32)) * (HGRN_DK ** -0.5)
            h = _Head(q, 1.0 - forget[:, lanes], p_ref[nh + hd, rows, :], g[:, lanes], low[:, lanes],
                      lanes, st_ref.at[hd])
            work.append((ci, rows, hd, h, _gated_intra(h, cum_ref.at[ci], lv)))
    for ci, rows, hd, h, o_intra in work:
        o = _gated_carry(h, o_intra, cum_ref.at[ci])
        o = _rms(o) * gn_ref[...] * _silu(p_ref[2 * nh + hd, rows, :].astype(F32))
        o_ref[rows, hd * LANE:(hd + 1) * LANE] = o.astype(o_ref.dtype)


def _scan_call(body, p, pg, consts, n_heads, dk, dv, name):
    s = p.shape[1]
    c = min(SCAN_CHUNK, s)
    n_chunks = min(SCAN_STEP_CHUNKS, s // c)
    tm = c * n_chunks
    const = lambda a: pl.BlockSpec(a.shape, lambda t: (0,) * a.ndim)
    blocks = lambda a: pl.BlockSpec((a.shape[0], tm, LANE), lambda t: (0, t, 0))
    return pl.pallas_call(
        body,
        grid=(s // tm,),
        in_specs=[blocks(p), blocks(pg)] + [const(a) for a in consts],
        out_specs=pl.BlockSpec((tm, n_heads * dv), lambda t: (t, 0)),
        out_shape=jax.ShapeDtypeStruct((s, n_heads * dv), BF16),
        scratch_shapes=[pltpu.VMEM((n_heads, dv, dk), F32),
                        pltpu.VMEM((n_chunks, c, n_heads * dk), F32)],
        compiler_params=_cparams(1),
        name=name,
    )(p, pg, *consts)


def _hgrn_scan(p, pf, lower_bound, g_norm):
    wcat, level = _scan_tables(min(SCAN_CHUNK, p.shape[1]))
    consts = (lower_bound.reshape(1, -1), g_norm.reshape(1, HGRN_DV), wcat, level)
    return _scan_call(_hgrn_kernel, p, pf, consts, HGRN_HEADS, HGRN_DK, HGRN_DV, "hgrn_scan")


def _log_sigmoid(z):
    return jnp.minimum(z, 0.0) - jnp.log(1.0 + jnp.exp(-jnp.abs(z)))


def _gla_kernel(p_ref, low_ref, wup_ref, b_ref, gn_ref, wcat_ref, lv_ref, o_ref, st_ref, cum_ref):
    _reset_state(st_ref)
    nh = GLA_HEADS
    nv = GLA_DV // LANE
    lv = lv_ref[...]
    work = []
    for ci, rows in enumerate(_chunk_rows(p_ref)):
        z = jnp.dot(low_ref[0, rows, :].astype(BF16), wup_ref[...], preferred_element_type=F32) + b_ref[...]
        g = _log_sigmoid(z) * (math.log2(math.e) / GLA_GATE_NORMALIZER)
        low = _gate_sums(g, wcat_ref, cum_ref.at[ci])
        for hd in range(nh):
            lanes = slice(hd * LANE, (hd + 1) * LANE)
            q = p_ref[hd, rows, :].astype(F32) * (GLA_DK ** -0.5)
            h = _Head(q, p_ref[nh + hd, rows, :].astype(F32), _cols(p_ref, 2 * nh + nv * hd, nv, rows),
                      g[:, lanes], low[:, lanes], lanes, st_ref.at[hd])
            work.append((ci, rows, hd, h, _gated_intra(h, cum_ref.at[ci], lv)))
    for ci, rows, hd, h, o_intra in work:
        o = _gated_carry(h, o_intra, cum_ref.at[ci])
        gate = _cols(p_ref, 2 * nh + nv * nh + nv * hd, nv, rows).astype(F32)
        o = _rms(o) * gn_ref[...] * _silu(gate)
        o_ref[rows, hd * GLA_DV:(hd + 1) * GLA_DV] = o.astype(o_ref.dtype)


def _gla_scan(p, p_low, w_up, b_gk, g_norm):
    wcat, level = _scan_tables(min(SCAN_CHUNK, p.shape[1]))
    consts = (w_up, b_gk.reshape(1, -1), g_norm.reshape(1, GLA_DV), wcat, level)
    return _scan_call(_gla_kernel, p, p_low, consts, GLA_HEADS, GLA_DK, GLA_DV, "gla_scan")


def _ret_kernel(p_ref, cos_ref, sin_ref, dm_ref, qd_ref, kd_ref, cd_ref, o_ref, st_ref):
    _reset_state(st_ref)
    nh = RET_HEADS
    nv = RET_DV // LANE
    half = RET_DK // 2
    work = []
    for rows in _chunk_rows(p_ref):
        cos = cos_ref[rows, :]
        sin = sin_ref[rows, :]
        qkv = []
        for hd in range(nh):
            q = p_ref[hd, rows, :].astype(F32)
            k = p_ref[nh + hd, rows, :].astype(F32)
            q = q * cos + pltpu.roll(q, half, 1) * sin
            k = (k * cos + pltpu.roll(k, half, 1) * sin) * (RET_DK ** -0.5)
            qkv.append((q, k, _cols(p_ref, 2 * nh + nv * hd, nv, rows)))
        for hd, (q, k, vb) in enumerate(qkv):
            scores = lax.dot_general(q.astype(BF16), k.astype(BF16), _NT,
                                     preferred_element_type=F32) * dm_ref[hd]
            work.append((rows, hd, q, k, vb, jnp.dot(scores.astype(BF16), vb, preferred_element_type=F32)))
    for n in range(0, len(work), nh):
        outs = []
        for rows, hd, q, k, vb, o in work[n:n + nh]:
            st = st_ref[hd]
            o = o + lax.dot_general((q * qd_ref[hd]).astype(BF16), st.astype(BF16), _NT,
                                    preferred_element_type=F32)
            st_ref[hd] = st * cd_ref[hd] + lax.dot_general(vb, (k * kd_ref[hd]).astype(BF16), _TN,
                                                           preferred_element_type=F32)
            outs.append(o)
        for (rows, hd, *_), o in zip(work[n:n + nh], outs):
            o = _rms(o) * _silu(_cols(p_ref, 2 * nh + nv * nh + nv * hd, nv, rows).astype(F32))
            o_ref[rows, hd * RET_DV:(hd + 1) * RET_DV] = o.astype(o_ref.dtype)


def _rope_angles(positions, dim):
    half = dim // 2
    inv_freq = 1.0 / (ROPE_THETA ** (jnp.arange(half, dtype=F32) / half))
    ang = positions.astype(F32)[:, None] * inv_freq
    return jnp.cos(ang), jnp.sin(ang)


def _rope_tables(cos, sin, pad_to):
    pad = jnp.zeros((cos.shape[0], pad_to // 2 - cos.shape[1]), F32)
    cos_t = jnp.concatenate([cos, pad, cos, pad], axis=1)
    sin_t = jnp.concatenate([-sin, pad, sin, pad], axis=1)
    return cos_t, sin_t


def _ret_scan(p, cos, sin):
    nb, s, _ = p.shape
    c = min(SCAN_CHUNK, s)
    h = RET_HEADS
    cos_t, sin_t = _rope_tables(cos, sin, LANE)
    log_gamma = jnp.log(1.0 - 2.0 ** (-5.0 - jnp.arange(h, dtype=F32)))
    idx = jnp.arange(c, dtype=F32)
    causal = jnp.tril(jnp.ones((c, c), dtype=bool))
    lg = log_gamma[:, None, None]
    dm = jnp.exp(jnp.where(causal, lg * (idx[:, None] - idx[None, :]), -jnp.inf))
    ones = jnp.ones((1, 1, LANE), F32)
    qd = jnp.exp(lg * (idx[None, :, None] + 1.0)) * ones
    kd = jnp.exp(lg * (c - 1.0 - idx[None, :, None])) * ones
    cd = jnp.exp(lg * float(c)) * ones
    tm = c * min(SCAN_STEP_CHUNKS, s // c)
    tab = pl.BlockSpec((tm, LANE), lambda t: (t, 0))
    const = lambda a: pl.BlockSpec(a.shape, lambda t: (0,) * a.ndim)
    return pl.pallas_call(
        _ret_kernel,
        grid=(s // tm,),
        in_specs=[pl.BlockSpec((nb, tm, LANE), lambda t: (0, t, 0)), tab, tab,
                  const(dm), const(qd), const(kd), const(cd)],
        out_specs=pl.BlockSpec((tm, h * RET_DV), lambda t: (t, 0)),
        out_shape=jax.ShapeDtypeStruct((s, h * RET_DV), BF16),
        scratch_shapes=[pltpu.VMEM((h, RET_DV, RET_DK), F32)],
        compiler_params=_cparams(1),
        name="ret_scan",
    )(p, cos_t, sin_t, dm, qd, kd, cd)


MLA_QK_PAD = 2 * LANE


def _mla_prep_kernel(h_ref, win_ref, gq_ref, gkv_ref, wuq_ref, wukt_ref, wuv_ref, gqn_ref, gkn_ref,
                     cos_ref, sin_ref, cos_t_ref, sin_t_ref, q_ref, k_ref, v_ref):
    nh = MLA_HEADS
    c = jnp.dot(h_ref[...], win_ref[...], preferred_element_type=F32)
    c_q = _rms(c[:, :MLA_Q_LORA]) * gq_ref[...]
    c_kv = (_rms(c[:, MLA_Q_LORA:MLA_Q_LORA + MLA_KV_LORA]) * gkv_ref[...]).astype(BF16)
    qf = jnp.dot(c_q.astype(BF16), wuq_ref[...], preferred_element_type=F32)
    vf = jnp.dot(c_kv, wuv_ref[...], preferred_element_type=F32)
    knt = lax.dot_general(wukt_ref[...], c_kv, _NT, preferred_element_type=F32)
    krt = c[:, MLA_Q_LORA + MLA_KV_LORA:].T
    kr_ss = jnp.sum(krt * krt, axis=0, keepdims=True)
    cos = cos_ref[...]
    sin = sin_ref[...]
    cos_t = cos_t_ref[...]
    sin_t = sin_t_ref[...]
    gqn = gqn_ref[...]
    gkn = gkn_ref[...]
    half = LANE // 2
    scale = MLA_QK ** -0.5 * math.log2(math.e)
    for hd in range(nh):
        q_n = qf[:, hd * LANE:(hd + 1) * LANE]
        q_r = qf[:, (nh + hd) * LANE:(nh + hd + 1) * LANE]
        ss = jnp.sum(q_n * q_n + q_r * q_r, axis=-1, keepdims=True)
        r = lax.rsqrt(ss * (1.0 / MLA_QK) + NORM_EPS)
        q_n = q_n * r * gqn[:, :LANE]
        q_r = q_r * r * gqn[:, LANE:]
        q_r = q_r * cos + pltpu.roll(q_r, half, 1) * sin
        q_ref[hd] = (jnp.concatenate([q_n, q_r], axis=1) * scale).astype(q_ref.dtype)
        k_n = knt[hd * LANE:(hd + 1) * LANE]
        ss = jnp.sum(k_n * k_n, axis=0, keepdims=True) + kr_ss
        r = lax.rsqrt(ss * (1.0 / MLA_QK) + NORM_EPS)
        k_n = k_n * r * gkn[:LANE]
        k_r = krt * r * gkn[LANE:]
        k_r = k_r * cos_t + jnp.concatenate([k_r[half:], k_r[:half]], axis=0) * sin_t
        k_ref[hd] = jnp.concatenate([k_n, k_r], axis=0).astype(k_ref.dtype)
        v_h = vf[:, hd * LANE:(hd + 1) * LANE]
        v_ref[hd] = jnp.concatenate([v_h, jnp.ones_like(v_h)], axis=1).astype(v_ref.dtype)


def _pad_rope_cols(w):
    half = MLA_ROPE // 2
    z = jnp.zeros(w.shape[:-1] + (LANE // 2 - half,), w.dtype)
    return jnp.concatenate([w[..., :half], z, w[..., half:], z], axis=-1)


def _mla_prep(h, cos, sin, w_in, g_q_lora, g_kv_lora, w_uq, w_ukv, g_qnorm, g_knorm):
    s = h.shape[0]
    tm = min(MLA_TM, s)
    nh = MLA_HEADS
    lat = MLA_Q_LORA + MLA_KV_LORA
    w_in_p = jnp.concatenate([w_in[:, :lat], _pad_rope_cols(w_in[:, lat:])], axis=1).astype(BF16)
    wq = w_uq.reshape(MLA_Q_LORA, nh, MLA_QK)
    wq_nope = wq[:, :, :MLA_NOPE].reshape(MLA_Q_LORA, nh * LANE)
    wq_rope = _pad_rope_cols(wq[:, :, MLA_NOPE:]).reshape(MLA_Q_LORA, nh * LANE)
    w_uq_p = jnp.concatenate([wq_nope, wq_rope], axis=1).astype(BF16)
    wkv = w_ukv.reshape(MLA_KV_LORA, nh, MLA_NOPE + MLA_V)
    w_uk_t = wkv[:, :, :MLA_NOPE].reshape(MLA_KV_LORA, nh * LANE).T.astype(BF16)
    w_uv = wkv[:, :, MLA_NOPE:].reshape(MLA_KV_LORA, nh * LANE).astype(BF16)
    pad_gain = lambda g: jnp.concatenate([g[:MLA_NOPE], _pad_rope_cols(g[MLA_NOPE:])])
    cos_t, sin_t = _rope_tables(cos, sin, LANE)
    const = lambda a: pl.BlockSpec(a.shape, lambda i: (0,) * a.ndim)
    args = (w_in_p, g_q_lora.reshape(1, -1), g_kv_lora.reshape(1, -1), w_uq_p, w_uk_t, w_uv,
            pad_gain(g_qnorm).reshape(1, -1), pad_gain(g_knorm).reshape(-1, 1))
    tab = pl.BlockSpec((tm, LANE), lambda i: (i, 0))
    tab_t = pl.BlockSpec((LANE, tm), lambda i: (0, i))
    return pl.pallas_call(
        _mla_prep_kernel,
        grid=(s // tm,),
        in_specs=([pl.BlockSpec((tm, D_MODEL), lambda i: (i, 0))] + [const(a) for a in args]
                  + [tab, tab, tab_t, tab_t]),
        out_specs=[pl.BlockSpec((nh, tm, MLA_QK_PAD), lambda i: (0, i, 0)),
                   pl.BlockSpec((nh, MLA_QK_PAD, tm), lambda i: (0, 0, i)),
                   pl.BlockSpec((nh, tm, 2 * MLA_V), lambda i: (0, i, 0))],
        out_shape=[jax.ShapeDtypeStruct((nh, s, MLA_QK_PAD), BF16),
                   jax.ShapeDtypeStruct((nh, MLA_QK_PAD, s), BF16),
                   jax.ShapeDtypeStruct((nh, s, 2 * MLA_V), BF16)],
        compiler_params=_cparams(1),
        name="mla_prep",
    )(h, *args, cos_t, sin_t, cos_t.T, sin_t.T)


def _flash_kernel(q_ref, qn_ref, k_ref, v_ref, o_ref, s_ref, mx_ref, m_ref, acc_ref, *, sub, tk):
    r = sub // tk
    i = pl.program_id(1)
    m_ref[...] = jnp.full(m_ref.shape, -jnp.inf, F32)
    acc_ref[...] = jnp.zeros(acc_ref.shape, F32)

    def scores(u, j, dst, src_ref=q_ref):
        off = pl.multiple_of(j * tk, tk)
        sc = jnp.dot(src_ref[0, u * sub:(u + 1) * sub, :], k_ref[0, :, pl.ds(off, tk)],
                     preferred_element_type=F32)
        s_ref[dst, u] = sc
        mx_ref[dst, u] = jnp.broadcast_to(jnp.max(sc, axis=-1, keepdims=True), (sub, LANE))

    def accumulate(u, j, src, shift):
        off = pl.multiple_of(j * tk, tk)
        s = s_ref[src, u]
        if shift is not None:
            row = lax.broadcasted_iota(jnp.int32, (sub, tk), 0)
            col = lax.broadcasted_iota(jnp.int32, (sub, tk), 1)
            s = jnp.where(col - row <= shift, s, -jnp.inf)
        m_prev = m_ref[u]
        m_new = jnp.maximum(m_prev, mx_ref[src, u] if shift is None else jnp.max(s, axis=-1, keepdims=True))
        p = jnp.exp2(s - jnp.concatenate([m_new] * (tk // LANE), axis=1)).astype(BF16)
        pv = jnp.dot(p, v_ref[0, pl.ds(off, tk), :], preferred_element_type=F32)
        alpha = jnp.exp2(m_prev - m_new)
        acc_ref[u] = jnp.concatenate([alpha] * (2 * MLA_V // LANE), axis=1) * acc_ref[u] + pv
        m_ref[u] = m_new

    @pl.when(i == 0)
    def _():
        for u in range(2):
            scores(u, 0, 0)

    def pair(a):
        for u in range(2):
            scores(u, a + 1, 1)
        for u in range(2):
            accumulate(u, a, 0, None)
        for u in range(2):
            scores(u, a + 2, 0)
        for u in range(2):
            accumulate(u, a + 1, 1, None)

    def body(jj, carry):
        pair(4 * jj)
        pair(4 * jj + 2)
        return carry

    lax.fori_loop(0, (r * i) // 2, body, 0)

    @pl.when((r * i) % 2 == 1)
    def _():
        pair(2 * (r * i) - 2)
    base = 2 * r * i
    for d in range(2 * r):
        if d + 1 < 2 * r:
            for u in range(2):
                if d + 1 < r * (u + 1):
                    scores(u, base + d + 1, (d + 1) % 2)
        else:
            for u in range(2):
                scores(u, 0, 0, qn_ref)
        for u in range(2):
            if d < r * (u + 1):
                accumulate(u, base + d, d % 2, None if d < r * u else (r * u - d) * tk)
    for u in range(2):
        acc = acc_ref[u]
        o_ref[u * sub:(u + 1) * sub, :] = (acc[:, :MLA_V] / acc[:, MLA_V:]).astype(o_ref.dtype)


def _flash(q, k, v):
    nh, s, _ = q.shape
    sub = min(ATTN_SUB, s // 2)
    tk = min(ATTN_TK, sub)
    tq = 2 * sub
    n_blocks = s // tq
    q_spec = lambda index: pl.BlockSpec((1, tq, MLA_QK_PAD), index)
    return pl.pallas_call(
        functools.partial(_flash_kernel, sub=sub, tk=tk),
        grid=(nh, n_blocks),
        in_specs=[q_spec(lambda h, i: (h, i, 0)),
                  q_spec(lambda h, i: (h, jnp.minimum(i + 1, n_blocks - 1), 0)),
                  pl.BlockSpec((1, MLA_QK_PAD, s), lambda h, i: (h, 0, 0)),
                  pl.BlockSpec((1, s, 2 * MLA_V), lambda h, i: (h, 0, 0))],
        out_specs=pl.BlockSpec((tq, MLA_V), lambda h, i: (i, h)),
        out_shape=jax.ShapeDtypeStruct((s, nh * MLA_V), BF16),
        scratch_shapes=[pltpu.VMEM((2, 2, sub, tk), F32),
                        pltpu.VMEM((2, 2, sub, LANE), F32),
                        pltpu.VMEM((2, sub, LANE), F32),
                        pltpu.VMEM((2, sub, 2 * MLA_V), F32)],
        compiler_params=_cparams(2),
        name="flash",
    )(q, q, k, v)


def _out_ffn_kernel(a_ref, x_ref, wo_ref, gf_ref, wgu_ref, wd_ref, gn_ref, xo_ref, *maybe_ho_ref):
    rows = x_ref.shape[0] // FFN_ROW_GROUPS
    parts = [slice(n * rows, (n + 1) * rows) for n in range(FFN_ROW_GROUPS)]
    x1 = [x_ref[r, :] + jnp.dot(a_ref[r, :], wo_ref[...], preferred_element_type=F32) for r in parts]
    h = [(_rms(v) * gf_ref[...]).astype(BF16) for v in x1]
    au = [jnp.dot(v, wgu_ref[...], preferred_element_type=F32) for v in h]
    p = [(_silu(v[:, :FFN_HIDDEN]) * v[:, FFN_HIDDEN:]).astype(BF16) for v in au]
    x2 = [v + jnp.dot(w, wd_ref[...], preferred_element_type=F32) for v, w in zip(x1, p)]
    for r, v in zip(parts, x2):
        xo_ref[r, :] = v
        for ho_ref in maybe_ho_ref:
            ho_ref[r, :] = (_rms(v) * gn_ref[...]).astype(ho_ref.dtype)


def _out_ffn(a, x, w_out, g_ffn, w_gate_up, w_down, layer, g_next, emit_next):
    s, d = x.shape
    tm = min(FFN_TM, s)
    row = lambda width: pl.BlockSpec((tm, width), lambda i: (i, 0))
    resident = lambda arr: pl.BlockSpec(arr.shape, lambda i: (0, 0), pipeline_mode=pl.Buffered(1))
    of_layer = lambda arr: pl.BlockSpec((None,) + arr.shape[1:], lambda i: (layer, 0, 0),
                                        pipeline_mode=pl.Buffered(1))
    gf = g_ffn.reshape(1, d)
    gn = g_next.reshape(1, d)
    n_out = 2 if emit_next else 1
    return pl.pallas_call(
        _out_ffn_kernel,
        grid=(s // tm,),
        in_specs=[row(a.shape[1]), row(d), resident(w_out), resident(gf), of_layer(w_gate_up),
                  of_layer(w_down), resident(gn)],
        out_specs=[row(d), row(d)][:n_out],
        out_shape=[jax.ShapeDtypeStruct((s, d), F32), jax.ShapeDtypeStruct((s, d), BF16)][:n_out],
        compiler_params=_cparams(1),
        name="out_ffn",
    )(a, x, w_out, gf, w_gate_up, w_down, gn)


def kernel(x, positions, norm_mix, norm_ffn, hgrn_w_in, hgrn_g_norm, hgrn_w_out, hgrn_lb_logits,
           gla_w_in, gla_w_gk_up, gla_b_gk, gla_g_norm, gla_w_out, ret_w_in, ret_w_out, mla_w_in,
           mla_g_q_lora, mla_g_kv_lora, mla_w_uq, mla_w_ukv, mla_g_qnorm, mla_g_knorm, mla_w_out,
           ffn_w_gate_up, ffn_w_down):
    b, s, d = x.shape
    depth = norm_mix.shape[0]
    lower_bounds = jnp.cumsum(jax.nn.softmax(hgrn_lb_logits.astype(F32), axis=0), axis=0)
    w_gate_up = ffn_w_gate_up.astype(BF16)
    w_down = ffn_w_down.astype(BF16)
    outs = []
    for bi in range(b):
        xs = x[bi]
        pos = positions[bi]
        cos, sin = _rope_angles(pos, RET_DK)
        step = RET_DK // MLA_ROPE
        h = _rmsnorm(xs, norm_mix[0])
        for i in range(depth):
            mixer, j = i % 4, i // 4
            if mixer == 0:
                w = hgrn_w_in[j]
                p = _proj(h, w, 3, lambda t: t + jnp.minimum(t, 1), BF16)
                pf = _proj(h, w, 1, lambda t: t + 1, F32)
                a = _hgrn_scan(p, pf, lower_bounds[i], hgrn_g_norm[j])
                w_out = hgrn_w_out[j]
            elif mixer == 1:
                main = 2 * GLA_KEY_DIM + 2 * GLA_VALUE_DIM
                pad = LANE - GLA_GATE_RANK
                p = _proj(h, gla_w_in[j], main // PROJ_TN, lambda t: t, BF16)
                w_low = jnp.pad(gla_w_in[j][:, main:], ((0, 0), (0, pad)))
                p_low = _proj(h, w_low, 1, lambda t: t, F32)
                w_up = jnp.pad(gla_w_gk_up[j], ((0, pad), (0, 0))).astype(BF16)
                a = _gla_scan(p, p_low, w_up, gla_b_gk[j], gla_g_norm[j])
                w_out = gla_w_out[j]
            elif mixer == 2:
                p = _proj(h, ret_w_in[j], ret_w_in.shape[2] // PROJ_TN, lambda t: t, BF16)
                a = _ret_scan(p, cos, sin)
                w_out = ret_w_out[j]
            else:
                q, k, v = _mla_prep(h, cos[:, ::step], sin[:, ::step], mla_w_in[j], mla_g_q_lora[j], mla_g_kv_lora[j],
                                    mla_w_uq[j], mla_w_ukv[j], mla_g_qnorm[j], mla_g_knorm[j])
                a = _flash(q, k, v)
                w_out = mla_w_out[j]
            last = i + 1 == depth
            res = _out_ffn(a, xs, w_out.astype(BF16), norm_ffn[i], w_gate_up, w_down, i,
                           norm_mix[0 if last else i + 1], not last)
            xs = res[0]
            h = None if last else res[1]
        outs.append(xs)
    return outs[0][None] if b == 1 else jnp.stack(outs, axis=0)
```
